```python
import math
import jax
import jax.numpy as jnp
from jax import lax
import numpy as np

D_MODEL = 1024
BATCH = 32
SEQ = 2048
DEPTH = 1

N_HEADS = 8
N_KV_GROUPS = 2
HEADS_PER_GROUP = N_HEADS // N_KV_GROUPS
HEAD_DIM = 64
ATTN_WIDTH = N_HEADS * HEAD_DIM
KV_WIDTH = N_KV_GROUPS * HEAD_DIM
N_KV_SLOTS = 6
ROPE_DIM = HEAD_DIM // 4
ROPE_THETA = 500000.0
CMP_BLOCK = 32
CMP_STRIDE = 16
CMP_HIDDEN = 4 * HEAD_DIM
SEL_BLOCK = 64
N_SEL_BLOCKS = 16
WINDOW = 512
WIN_QBLOCK = 128
SEL_QBLOCK = 16
CONV_WIDTH = 512
CONV_KERNEL = 3
D_FF = 2816
N_BRANCHES = 2
W_IN_SIZES = (ATTN_WIDTH, N_KV_SLOTS * KV_WIDTH, 3 * N_HEADS, 3 * CONV_WIDTH, N_BRANCHES * D_MODEL)
W_IN_COLS = ATTN_WIDTH + N_KV_SLOTS * KV_WIDTH + 3 * N_HEADS + 3 * CONV_WIDTH + N_BRANCHES * D_MODEL

NORM_EPS = 1e-6
MASK_VALUE = -1e30
FORCE_SCORE = 1e4

kernel_name = "hybrid_nsa_shortconv_macaron"


def rms_norm(x, g):
    x32 = x.astype(jnp.float32)
    y = x32 * lax.rsqrt(jnp.mean(x32 * x32, axis=-1, keepdims=True) + NORM_EPS)
    return y.astype(x.dtype) * g


def swiglu(x, w_gate, w_up, w_down):
    return (jax.nn.silu(x @ w_gate) * (x @ w_up)) @ w_down


def partial_rope(x, pos):
    half = ROPE_DIM // 2
    inv_freq = ROPE_THETA ** (-jnp.arange(0, ROPE_DIM, 2, dtype=jnp.float32) / ROPE_DIM)
    ang = pos.astype(jnp.float32)[:, None] * inv_freq[None, :]
    cos = jnp.cos(ang)[:, None, :]
    sin = jnp.sin(ang)[:, None, :]
    xr = x[..., :ROPE_DIM].astype(jnp.float32)
    x1, x2 = xr[..., :half], xr[..., half:]
    rot = jnp.concatenate([x1 * cos - x2 * sin, x2 * cos + x1 * sin], axis=-1)
    return jnp.concatenate([rot.astype(x.dtype), x[..., ROPE_DIM:]], axis=-1)


def compress(t, pe, w1, w2):
    b, s, g, d = t.shape
    n_chunks = s // CMP_STRIDE
    n_per = CMP_BLOCK // CMP_STRIDE
    nb = n_chunks - n_per + 1
    chunks = t.reshape(b, n_chunks, CMP_STRIDE, g, d)
    blocks = jnp.concatenate([chunks[:, j:j + nb] for j in range(n_per)], axis=2)
    blocks = blocks + pe[None, None, :, None, :]
    flat = blocks.transpose(0, 1, 3, 2, 4).reshape(b, nb, g, CMP_BLOCK * d)
    return jax.nn.silu(flat @ w1) @ w2


def compressed_attention(q, k_c, v_c, pos):
    nb = k_c.shape[2]
    cmp_end = jnp.arange(nb) * CMP_STRIDE + CMP_BLOCK - 1
    valid = cmp_end[None, :] <= pos[:, None]
    s = jnp.einsum('bghsd,bgnd->bghsn', q, k_c).astype(jnp.float32) * (HEAD_DIM ** -0.5)
    p = jax.nn.softmax(jnp.where(valid, s, MASK_VALUE), axis=-1) * valid.astype(jnp.float32)
    o = jnp.einsum('bghsn,bgnd->bghsd', p.astype(v_c.dtype), v_c)
    return o, p


def select_blocks(p_c, pos):
    s = pos.shape[0]
    nb = p_c.shape[-1]
    nsb = s // SEL_BLOCK
    k_eff = min(N_SEL_BLOCKS, nsb)
    cs = jnp.arange(nb) * CMP_STRIDE
    ce = cs + CMP_BLOCK
    ss = jnp.arange(nsb) * SEL_BLOCK
    se = ss + SEL_BLOCK
    overlap = ((cs[:, None] < se[None, :]) & (ce[:, None] > ss[None, :])).astype(jnp.float32)
    imp = jnp.einsum('bghsn,nj->bgsj', p_c, overlap)
    j = jnp.arange(nsb)[None, :]
    cur = (pos // SEL_BLOCK)[:, None]
    forced = (j == 0) | (j == cur) | (j == cur - 1)
    future = ss[None, :] > pos[:, None]
    imp = jnp.where(forced, FORCE_SCORE, jnp.where(future, -1.0, imp))
    _, idx = lax.top_k(imp, k_eff)
    return idx


def selected_attention(q, k_s, v_s, sel_idx):
    b, g, hp, s, d = q.shape
    kk = sel_idx.shape[-1]
    nsb = s // SEL_BLOCK
    n_qc = s // SEL_QBLOCK
    k_blk = k_s.reshape(b, g, nsb, SEL_BLOCK, d)
    v_blk = v_s.reshape(b, g, nsb, SEL_BLOCK, d)
    q_ch = q.reshape(b, g, hp, n_qc, SEL_QBLOCK, d).transpose(3, 0, 1, 2, 4, 5)
    i_ch = sel_idx.reshape(b, g, n_qc, SEL_QBLOCK, kk).transpose(2, 0, 1, 3, 4)
    starts = jnp.arange(n_qc) * SEL_QBLOCK
    b_ix = jnp.arange(b)[:, None, None, None]
    g_ix = jnp.arange(g)[None, :, None, None]

    def step(args):
        qc, ic, st = args
        kg = k_blk[b_ix, g_ix, ic]
        vg = v_blk[b_ix, g_ix, ic]
        sc = jnp.einsum('bghqd,bgqkld->bghqkl', qc, kg).astype(jnp.float32) * (HEAD_DIM ** -0.5)
        kpos = ic[..., None] * SEL_BLOCK + jnp.arange(SEL_BLOCK)
        qpos = st + jnp.arange(SEL_QBLOCK)
        valid = kpos <= qpos[None, None, :, None, None]
        sc = jnp.where(valid[:, :, None], sc, MASK_VALUE).reshape(b, g, hp, SEL_QBLOCK, kk * SEL_BLOCK)
        p = jax.nn.softmax(sc, axis=-1).reshape(b, g, hp, SEL_QBLOCK, kk, SEL_BLOCK)
        return jnp.einsum('bghqkl,bgqkld->bghqd', p.astype(vg.dtype), vg)

    o = lax.map(step, (q_ch, i_ch, starts))
    return o.transpose(1, 2, 3, 0, 4, 5).reshape(b, g, hp, s, d)


def window_attention(q, k_w, v_w):
    b, g, hp, s, d = q.shape
    n_wb = s // WIN_QBLOCK
    span = WINDOW + WIN_QBLOCK
    k_pad = jnp.pad(k_w, ((0, 0), (0, 0), (WINDOW, 0), (0, 0)))
    v_pad = jnp.pad(v_w, ((0, 0), (0, 0), (WINDOW, 0), (0, 0)))
    q_bl = q.reshape(b, g, hp, n_wb, WIN_QBLOCK, d).transpose(3, 0, 1, 2, 4, 5)
    starts = jnp.arange(n_wb) * WIN_QBLOCK

    def step(args):
        qb, st = args
        kb = lax.dynamic_slice_in_dim(k_pad, st, span, axis=2)
        vb = lax.dynamic_slice_in_dim(v_pad, st, span, axis=2)
        sc = jnp.einsum('bghqd,bgkd->bghqk', qb, kb).astype(jnp.float32) * (HEAD_DIM ** -0.5)
        kpos = st - WINDOW + jnp.arange(span)
        qpos = st + jnp.arange(WIN_QBLOCK)
        diff = qpos[:, None] - kpos[None, :]
        valid = (kpos[None, :] >= 0) & (diff >= 0) & (diff < WINDOW)
        p = jax.nn.softmax(jnp.where(valid, sc, MASK_VALUE), axis=-1)
        return jnp.einsum('bghqk,bgkd->bghqd', p.astype(vb.dtype), vb)

    o = lax.map(step, (q_bl, starts))
    return o.transpose(1, 2, 3, 0, 4, 5).reshape(b, g, hp, s, d)


def nsa_mixer(q, kv, gate_logits, q_norm_g, k_norm_g, cmp_pe_k, cmp_pe_v,
              cmp_k_w1, cmp_k_w2, cmp_v_w1, cmp_v_w2):
    b, s, _ = q.shape
    pos = jnp.arange(s)
    q = partial_rope(rms_norm(q.reshape(b, s, N_HEADS, HEAD_DIM), q_norm_g), pos)
    q = q.reshape(b, s, N_KV_GROUPS, HEADS_PER_GROUP, HEAD_DIM).transpose(0, 2, 3, 1, 4)
    kv = kv.reshape(b, s, N_KV_SLOTS, N_KV_GROUPS, HEAD_DIM)
    k_c = rms_norm(compress(kv[:, :, 0], cmp_pe_k, cmp_k_w1, cmp_k_w2), k_norm_g[0]).transpose(0, 2, 1, 3)
    v_c = compress(kv[:, :, 1], cmp_pe_v, cmp_v_w1, cmp_v_w2).transpose(0, 2, 1, 3)
    k_s = partial_rope(rms_norm(kv[:, :, 2], k_norm_g[1]), pos).transpose(0, 2, 1, 3)
    v_s = kv[:, :, 3].transpose(0, 2, 1, 3)
    k_w = partial_rope(rms_norm(kv[:, :, 4], k_norm_g[2]), pos).transpose(0, 2, 1, 3)
    v_w = kv[:, :, 5].transpose(0, 2, 1, 3)

    o_c, p_c = compressed_attention(q, k_c, v_c, pos)
    sel_idx = select_blocks(p_c, pos)
    o_s = selected_attention(q, k_s, v_s, sel_idx)
    o_w = window_attention(q, k_w, v_w)

    gts = jax.nn.sigmoid(gate_logits.reshape(b, s, 3, N_KV_GROUPS, HEADS_PER_GROUP).astype(jnp.float32))
    gts = gts.astype(q.dtype).transpose(2, 0, 3, 4, 1)[..., None]
    o = gts[0] * o_c + gts[1] * o_s + gts[2] * o_w
    return o.transpose(0, 3, 1, 2, 4).reshape(b, s, ATTN_WIDTH)


def short_conv_mixer(cv):
    return jnp.split(cv, 3, axis=-1)


def setup_inputs(seed: int = 0) -> dict:
    key = jax.random.key(seed)
    ks = jax.random.split(key, 24)
    L = DEPTH

    def nrm(k, shape, fan_in):
        return jax.random.normal(k, shape, jnp.float32) * (fan_in ** -0.5)

    def gain(k, shape):
        return 1.0 + 0.02 * jax.random.normal(k, shape, jnp.float32)

    return {
        "x": jax.random.normal(ks[0], (BATCH, SEQ, D_MODEL), jnp.float32),
        "ffn1_norm_g": gain(ks[1], (L, D_MODEL)),
        "ffn1_w_gate": nrm(ks[2], (L, D_MODEL, D_FF), D_MODEL),
        "ffn1_w_up": nrm(ks[3], (L, D_MODEL, D_FF), D_MODEL),
        "ffn1_w_down": nrm(ks[4], (L, D_FF, D_MODEL), D_FF),
        "mix_norm_g": gain(ks[5], (L, D_MODEL)),
        "w_in": nrm(ks[6], (L, D_MODEL, W_IN_COLS), D_MODEL),
        "q_norm_g": gain(ks[7], (L, HEAD_DIM)),
        "k_norm_g": gain(ks[8], (L, 3, HEAD_DIM)),
        "cmp_pe_k": 0.1 * jax.random.normal(ks[9], (L, CMP_BLOCK, HEAD_DIM), jnp.float32),
        "cmp_pe_v": 0.1 * jax.random.normal(ks[10], (L, CMP_BLOCK, HEAD_DIM), jnp.float32),
        "cmp_k_w1": nrm(ks[11], (L, CMP_BLOCK * HEAD_DIM, CMP_HIDDEN), CMP_BLOCK * HEAD_DIM),
        "cmp_k_w2": nrm(ks[12], (L, CMP_HIDDEN, HEAD_DIM), CMP_HIDDEN),
        "cmp_v_w1": nrm(ks[13], (L, CMP_BLOCK * HEAD_DIM, CMP_HIDDEN), CMP_BLOCK * HEAD_DIM),
        "cmp_v_w2": nrm(ks[14], (L, CMP_HIDDEN, HEAD_DIM), CMP_HIDDEN),
        "conv_w": nrm(ks[15], (L, CONV_KERNEL, CONV_WIDTH), CONV_KERNEL),
        "w_attn_branch": nrm(ks[16], (L, ATTN_WIDTH, D_MODEL), ATTN_WIDTH),
        "w_conv_branch": nrm(ks[17], (L, CONV_WIDTH, D_MODEL), CONV_WIDTH),
        "w_out": nrm(ks[18], (L, D_MODEL, D_MODEL), D_MODEL),
        "ffn2_norm_g": gain(ks[19], (L, D_MODEL)),
        "ffn2_w_gate": nrm(ks[20], (L, D_MODEL, D_FF), D_MODEL),
        "ffn2_w_up": nrm(ks[21], (L, D_MODEL, D_FF), D_MODEL),
        "ffn2_w_down": nrm(ks[22], (L, D_FF, D_MODEL), D_FF),
    }


def reference(x, ffn1_norm_g, ffn1_w_gate, ffn1_w_up, ffn1_w_down, mix_norm_g, w_in,
              q_norm_g, k_norm_g, cmp_pe_k, cmp_pe_v, cmp_k_w1, cmp_k_w2, cmp_v_w1, cmp_v_w2,
              conv_w, w_attn_branch, w_conv_branch, w_out,
              ffn2_norm_g, ffn2_w_gate, ffn2_w_up, ffn2_w_down):
    split_at = [int(v) for v in np.cumsum(W_IN_SIZES)[:-1]]
    for l in range(DEPTH):
        x = x + 0.5 * swiglu(rms_norm(x, ffn1_norm_g[l]), ffn1_w_gate[l], ffn1_w_up[l], ffn1_w_down[l])

        h = rms_norm(x, mix_norm_g[l])
        proj = h @ w_in[l]
        q, kv, nsa_gl, cv, merge_gl = jnp.split(proj, split_at, axis=-1)

        a = nsa_mixer(q, kv, nsa_gl, q_norm_g[l], k_norm_g[l], cmp_pe_k[l], cmp_pe_v[l],
                      cmp_k_w1[l], cmp_k_w2[l], cmp_v_w1[l], cmp_v_w2[l])

        gate_b, gate_c, u = short_conv_mixer(cv)
        conv = lax.conv_general_dilated(
            gate_c * u, conv_w[l][:, None, :], window_strides=(1,),
            padding=[(CONV_KERNEL - 1, 0)], dimension_numbers=('NWC', 'WIO', 'NWC'),
            feature_group_count=CONV_WIDTH)
        c = gate_b * conv

        mg = jax.nn.sigmoid(merge_gl.astype(jnp.float32)).astype(x.dtype)
        g_a, g_c = mg[..., :D_MODEL], mg[..., D_MODEL:]
        merged = g_a * (a @ w_attn_branch[l]) + g_c * (c @ w_conv_branch[l])
        x = x + merged @ w_out[l]

        x = x + 0.5 * swiglu(rms_norm(x, ffn2_norm_g[l]), ffn2_w_gate[l], ffn2_w_up[l], ffn2_w_down[l])
    return x
```

```python
import functools

import numpy as np
import jax
import jax.numpy as jnp
from jax import lax
from jax.experimental import pallas as pl
from jax.experimental.pallas import tpu as pltpu

D_MODEL = 1024
N_HEADS = 8
N_GROUPS = 2
HEADS_PER_GROUP = N_HEADS // N_GROUPS
HEAD_DIM = 64
ATTN_WIDTH = N_HEADS * HEAD_DIM
KV_WIDTH = N_GROUPS * HEAD_DIM
N_KV_SLOTS = 6
ROPE_DIM = HEAD_DIM // 4
ROPE_HALF = ROPE_DIM // 2
ROPE_THETA = 500000.0
CMP_BLOCK = 32
CMP_STRIDE = 16
CMP_HIDDEN = 4 * HEAD_DIM
SEL_BLOCK = 64
N_SEL_BLOCKS = 16
WINDOW = 512
CONV_WIDTH = 512
CONV_KERNEL = 3
D_FF = 2816
NORM_EPS = 1e-6
MASK_VALUE = -1e30
FORCE_SCORE = 1e4
N_GATES = 3 * N_HEADS

LANES = 128
SLAB = LANES
Q_PAD = N_HEADS * SLAB
KV_PAD = N_GROUPS * SLAB
SEL_BIAS = -float(2 ** 30)
SEL_LANE0 = HEAD_DIM

TM = 512
TQ = 256
FF_CHUNK = 1408
VMEM_LIMIT = 56 * 1024 * 1024

F32 = jnp.float32
BF16 = jnp.bfloat16

_NT = (((1,), (1,)), ((), ()))


def _const_spec(shape):
    return pl.BlockSpec(shape, lambda *_: (0,) * len(shape), pipeline_mode=pl.Buffered(1))


def _params(n_axes):
    return pltpu.CompilerParams(dimension_semantics=("parallel",) * n_axes, vmem_limit_bytes=VMEM_LIMIT)


def _rms_rows(x, g):
    ms = jnp.mean(x * x, axis=-1, keepdims=True)
    return x * lax.rsqrt(ms + NORM_EPS) * g


def _ffn_kernel(x_ref, g_ref, wg_ref, wu_ref, wd_ref, o_ref):
    x = x_ref[...]
    h = _rms_rows(x, g_ref[...]).astype(BF16)
    acc = jnp.zeros(x.shape, F32)
    for c in range(D_FF // FF_CHUNK):
        sl = slice(c * FF_CHUNK, (c + 1) * FF_CHUNK)
        gate = jnp.dot(h, wg_ref[:, sl], preferred_element_type=F32)
        up = jnp.dot(h, wu_ref[:, sl], preferred_element_type=F32)
        act = (gate * jax.nn.sigmoid(gate) * up).astype(BF16)
        acc = acc + jnp.dot(act, wd_ref[sl, :], preferred_element_type=F32)
    o_ref[...] = x + 0.5 * acc


def _ffn(x, g, wg, wu, wd):
    n = x.shape[0]
    return pl.pallas_call(
        _ffn_kernel,
        grid=(n // TM,),
        in_specs=[
            pl.BlockSpec((TM, D_MODEL), lambda i: (i, 0)),
            _const_spec((1, D_MODEL)),
            _const_spec((D_MODEL, D_FF)),
            _const_spec((D_MODEL, D_FF)),
            _const_spec((D_FF, D_MODEL)),
        ],
        out_specs=pl.BlockSpec((TM, D_MODEL), lambda i: (i, 0)),
        out_shape=jax.ShapeDtypeStruct((n, D_MODEL), F32),
        compiler_params=_params(1),
        name="ffn",
    )(x, g, wg, wu, wd)


_C_Q = 0
_C_KSEL = _C_Q + Q_PAD
_C_KWIN = _C_KSEL + KV_PAD
_C_VSEL = _C_KWIN + KV_PAD
_C_VWIN = _C_VSEL + KV_PAD
_C_KCMP = _C_VWIN + KV_PAD
_C_VCMP = _C_KCMP + KV_WIDTH
_C_GATE = _C_VCMP + KV_WIDTH
_C_CONV = _C_GATE + KV_PAD
_C_END = _C_CONV + 3 * CONV_WIDTH


def _norm_rope_slab(x, g, cos, sin_lo, sin_hi):
    ms = jnp.sum(x * x, axis=-1, keepdims=True) * (1.0 / HEAD_DIM)
    y = x * lax.rsqrt(ms + NORM_EPS) * g
    return y * cos + pltpu.roll(y, ROPE_HALF, 1) * sin_hi + pltpu.roll(y, SLAB - ROPE_HALF, 1) * sin_lo


def _inproj_kernel(x_ref, g_ref, w_ref, qg_ref, kg_ref, cos_ref, slo_ref, shi_ref, oh_ref,
                   q_ref, ksel_ref, kwin_ref, vsel_ref, vwin_ref, kcmp_ref, vcmp_ref, gate_ref, cu_ref, bg_ref):
    h = _rms_rows(x_ref[...], g_ref[...]).astype(BF16)
    cos, slo, shi = cos_ref[...], slo_ref[...], shi_ref[...]

    def proj(c0, width):
        return jnp.dot(h, w_ref[:, c0:c0 + width], preferred_element_type=F32)

    q = proj(_C_Q, Q_PAD)
    qg = qg_ref[...]
    for hd in range(N_HEADS):
        sl = slice(hd * SLAB, (hd + 1) * SLAB)
        y = _norm_rope_slab(q[:, sl], qg, cos, slo, shi) * (HEAD_DIM ** -0.5)
        q_ref[:, sl] = y.astype(BF16)

    ksel = proj(_C_KSEL, KV_PAD)
    kwin = proj(_C_KWIN, KV_PAD)
    oh = oh_ref[...]
    for g in range(N_GROUPS):
        sl = slice(g * SLAB, (g + 1) * SLAB)
        ksel_ref[:, sl] = (_norm_rope_slab(ksel[:, sl], kg_ref[0:1, :], cos, slo, shi) + oh).astype(BF16)
        kwin_ref[:, sl] = _norm_rope_slab(kwin[:, sl], kg_ref[1:2, :], cos, slo, shi).astype(BF16)

    vsel_ref[...] = proj(_C_VSEL, KV_PAD).astype(BF16)
    vwin_ref[...] = proj(_C_VWIN, KV_PAD).astype(BF16)
    kcmp_ref[...] = proj(_C_KCMP, KV_WIDTH)
    vcmp_ref[...] = proj(_C_VCMP, KV_WIDTH)
    gate_ref[...] = jax.nn.sigmoid(proj(_C_GATE, KV_PAD))
    conv = proj(_C_CONV, 3 * CONV_WIDTH)
    bg_ref[...] = conv[:, :CONV_WIDTH]
    cu_ref[...] = conv[:, CONV_WIDTH:2 * CONV_WIDTH] * conv[:, 2 * CONV_WIDTH:]


def _inproj(x1, g, w_b, qg, kg, cos, slo, shi, oh, seq):
    n = x1.shape[0]
    tiles_per_seq = seq // TM
    row = lambda i: (i, 0)
    pos = lambda i: (i % tiles_per_seq, 0)
    out_widths = [(Q_PAD, BF16), (KV_PAD, BF16), (KV_PAD, BF16), (KV_PAD, BF16), (KV_PAD, BF16),
                  (KV_WIDTH, F32), (KV_WIDTH, F32), (KV_PAD, F32), (CONV_WIDTH, F32), (CONV_WIDTH, F32)]
    return pl.pallas_call(
        _inproj_kernel,
        grid=(n // TM,),
        in_specs=[
            pl.BlockSpec((TM, D_MODEL), row),
            _const_spec((1, D_MODEL)),
            _const_spec((D_MODEL, _C_END)),
            _const_spec((1, SLAB)),
            _const_spec((2, SLAB)),
            pl.BlockSpec((TM, SLAB), pos),
            pl.BlockSpec((TM, SLAB), pos),
            pl.BlockSpec((TM, SLAB), pos),
            pl.BlockSpec((TM, SLAB), pos),
        ],
        out_specs=[pl.BlockSpec((TM, w), row) for w, _ in out_widths],
        out_shape=[jax.ShapeDtypeStruct((n, w), dt) for w, dt in out_widths],
        compiler_params=_params(1),
        name="inproj",
    )(x1, g, w_b, qg, kg, cos, slo, shi, oh)


N_CHUNKS_ROW = CMP_STRIDE * KV_WIDTH


def _compress_kernel(kc_ref, vc_ref, pek_ref, pev_ref, wk1_ref, wv1_ref, wk2_ref, wv2_ref, kg_ref,
                     ko_ref, vo_ref):
    def branch(r_ref, pe_ref, w1_ref, w2_ref):
        r = r_ref[0]
        top = jnp.dot((r + pe_ref[0:1, :]).astype(BF16), w1_ref[0], preferred_element_type=F32)
        bot = jnp.dot((r + pe_ref[1:2, :]).astype(BF16), w1_ref[1], preferred_element_type=F32)
        hid = top + pltpu.roll(bot, bot.shape[0] - 1, 0)
        hid = (hid * jax.nn.sigmoid(hid)).astype(BF16)
        return [jnp.dot(hid[:, g * CMP_HIDDEN:(g + 1) * CMP_HIDDEN], w2_ref[...], preferred_element_type=F32)
                for g in range(N_GROUPS)]

    k = branch(kc_ref, pek_ref, wk1_ref, wk2_ref)
    v = branch(vc_ref, pev_ref, wv1_ref, wv2_ref)
    for g in range(N_GROUPS):
        ms = jnp.sum(k[g] * k[g], axis=-1, keepdims=True) * (1.0 / HEAD_DIM)
        ko_ref[0, g] = (k[g] * lax.rsqrt(ms + NORM_EPS) * kg_ref[...]).astype(BF16)
        vo_ref[0, g] = v[g].astype(BF16)


def _compress(kc, vc, pek, pev, wk1, wv1, wk2, wv2, kg):
    b, n_chunks, _ = kc.shape
    blk = pl.BlockSpec((1, n_chunks, N_CHUNKS_ROW), lambda i: (i, 0, 0))
    out = pl.BlockSpec((1, N_GROUPS, n_chunks, SLAB), lambda i: (i, 0, 0, 0))
    return pl.pallas_call(
        _compress_kernel,
        grid=(b,),
        in_specs=[blk, blk,
                  _const_spec((2, N_CHUNKS_ROW)), _const_spec((2, N_CHUNKS_ROW)),
                  _const_spec((2, N_CHUNKS_ROW, N_GROUPS * CMP_HIDDEN)),
                  _const_spec((2, N_CHUNKS_ROW, N_GROUPS * CMP_HIDDEN)),
                  _const_spec((CMP_HIDDEN, SLAB)), _const_spec((CMP_HIDDEN, SLAB)),
                  _const_spec((1, SLAB))],
        out_specs=[out, out],
        out_shape=[jax.ShapeDtypeStruct((b, N_GROUPS, n_chunks, SLAB), BF16)] * 2,
        compiler_params=_params(1),
        name="compress",
    )(kc, vc, pek, pev, wk1, wv1, wk2, wv2, kg)


def _softmax_step(carry, s, v):
    m, l, acc = carry
    m_new = jnp.maximum(m, jnp.max(s, axis=-1, keepdims=True))
    alpha = jnp.exp(m - m_new)
    p = jnp.exp(s - m_new)
    l = alpha * l + jnp.sum(p, axis=-1, keepdims=True)
    acc = alpha * acc + jnp.dot(p.astype(BF16), v, preferred_element_type=F32)
    return m_new, l, acc


def _attn_kernel(q_ref, kc_ref, vc_ref, ksel_ref, vsel_ref, kwin_ref, vwin_ref, gate_ref, ovl_ref, o_ref,
                 *, n_sel_blocks, k_eff):
    qt = pl.program_id(2)
    q0 = qt * TQ
    n_cmp = kc_ref.shape[0]

    pos_col = q0 + lax.broadcasted_iota(jnp.int32, (TQ, n_cmp), 0)
    cmp_end = lax.broadcasted_iota(jnp.int32, (TQ, n_cmp), 1) * CMP_STRIDE + (CMP_BLOCK - 1)
    valid_c = cmp_end <= pos_col
    kc = kc_ref[...]
    vc = vc_ref[...]
    o_cmp = []
    p_sum = jnp.zeros((TQ, n_cmp), F32)
    for h in range(HEADS_PER_GROUP):
        qh = q_ref[:, h * SLAB:(h + 1) * SLAB]
        s = lax.dot_general(qh, kc, _NT, preferred_element_type=F32)
        s = jnp.where(valid_c, s, MASK_VALUE)
        e = jnp.exp(s - jnp.max(s, axis=-1, keepdims=True))
        p = jnp.where(valid_c, e / jnp.sum(e, axis=-1, keepdims=True), 0.0)
        p_sum = p_sum + p
        o_cmp.append(jnp.dot(p.astype(BF16), vc, preferred_element_type=F32))

    p_hi = p_sum.astype(BF16)
    p_lo = (p_sum - p_hi.astype(F32)).astype(BF16)
    ovl = ovl_ref[...]
    imp = (lax.dot_general(ovl, p_hi, _NT, preferred_element_type=F32)
           + lax.dot_general(ovl, p_lo, _NT, preferred_element_type=F32))
    j_idx = lax.broadcasted_iota(jnp.int32, (n_sel_blocks, TQ), 0)
    pos_t = q0 + lax.broadcasted_iota(jnp.int32, (n_sel_blocks, TQ), 1)
    cur = pos_t // SEL_BLOCK
    imp = jnp.where(j_idx * SEL_BLOCK > pos_t, -1.0, imp)
    imp = jnp.where(j_idx == 0, FORCE_SCORE, imp)
    imp = jnp.where(j_idx == cur, FORCE_SCORE, imp)
    imp = jnp.where(j_idx == cur - 1, FORCE_SCORE, imp)
    rank = jnp.zeros((n_sel_blocks, TQ), F32)
    for jp in range(n_sel_blocks):
        row = imp[jp:jp + 1, :]
        tie = jnp.where(j_idx > jp, 1.0, 0.0)
        rank = rank + jnp.where(row > imp, 1.0, jnp.where(row == imp, tie, 0.0))
    bias_t = jnp.where(rank < k_eff, 0.0, SEL_BIAS)
    bias_t = jnp.concatenate([jnp.zeros((SEL_LANE0, TQ), F32), bias_t,
                              jnp.zeros((SLAB - SEL_LANE0 - n_sel_blocks, TQ), F32)], axis=0)
    bias = bias_t.T

    row_q = lax.broadcasted_iota(jnp.int32, (TQ, TQ), 0)
    col_k = lax.broadcasted_iota(jnp.int32, (TQ, TQ), 1)
    diff0 = row_q - col_k

    gates = gate_ref[...]
    for h in range(HEADS_PER_GROUP):
        qh = q_ref[:, h * SLAB:(h + 1) * SLAB]
        init = (jnp.full((TQ, 1), MASK_VALUE, F32), jnp.zeros((TQ, 1), F32), jnp.zeros((TQ, SLAB), F32))

        qa = (qh.astype(F32) + bias).astype(BF16)

        def sel_body(kt, carry):
            k0 = pl.multiple_of(kt * TQ, TQ)
            s = lax.dot_general(qa, ksel_ref[pl.ds(k0, TQ), :], _NT, preferred_element_type=F32)
            return _softmax_step(carry, s, vsel_ref[pl.ds(k0, TQ), :])

        carry = lax.fori_loop(0, qt, sel_body, init)
        k0 = pl.multiple_of(q0, TQ)
        s = lax.dot_general(qa, ksel_ref[pl.ds(k0, TQ), :], _NT, preferred_element_type=F32)
        s = jnp.where(diff0 >= 0, s, MASK_VALUE)
        _, l_s, acc_s = _softmax_step(carry, s, vsel_ref[pl.ds(k0, TQ), :])

        carry = init
        for back in range(WINDOW // TQ, -1, -1):
            kt_raw = qt - back
            k0 = pl.multiple_of(jnp.maximum(kt_raw, 0) * TQ, TQ)
            s = lax.dot_general(qh, kwin_ref[pl.ds(k0, TQ), :], _NT, preferred_element_type=F32)
            diff = diff0 + back * TQ
            ok = jnp.where(diff >= 0, jnp.where(diff < WINDOW, 1.0, 0.0), 0.0) * jnp.where(kt_raw >= 0, 1.0, 0.0)
            s = jnp.where(ok > 0.0, s, MASK_VALUE)
            carry = _softmax_step(carry, s, vwin_ref[pl.ds(k0, TQ), :])
        _, l_w, acc_w = carry

        o = (gates[:, h:h + 1] * o_cmp[h]
             + gates[:, HEADS_PER_GROUP + h:HEADS_PER_GROUP + h + 1] * (acc_s / l_s)
             + gates[:, 2 * HEADS_PER_GROUP + h:2 * HEADS_PER_GROUP + h + 1] * (acc_w / l_w))
        o_ref[:, h * SLAB:(h + 1) * SLAB] = o.astype(BF16)


def _attention(q, kc, vc, ksel, vsel, kwin, vwin, gates, ovl, batch, seq):
    n = q.shape[0]
    n_cmp = kc.shape[2]
    n_sel_blocks = seq // SEL_BLOCK
    tiles = seq // TQ
    cmp_spec = pl.BlockSpec((None, None, n_cmp, SLAB), lambda b, g, t: (b, g, 0, 0))
    kv_spec = pl.BlockSpec((seq, SLAB), lambda b, g, t: (b, g))
    return pl.pallas_call(
        functools.partial(_attn_kernel, n_sel_blocks=n_sel_blocks, k_eff=min(N_SEL_BLOCKS, n_sel_blocks)),
        grid=(batch, N_GROUPS, tiles),
        in_specs=[
            pl.BlockSpec((TQ, HEADS_PER_GROUP * SLAB), lambda b, g, t: (b * tiles + t, g)),
            cmp_spec, cmp_spec, kv_spec, kv_spec, kv_spec, kv_spec,
            pl.BlockSpec((TQ, SLAB), lambda b, g, t: (b * tiles + t, g)),
            _const_spec((n_sel_blocks, n_cmp)),
        ],
        out_specs=pl.BlockSpec((TQ, HEADS_PER_GROUP * SLAB), lambda b, g, t: (b * tiles + t, g)),
        out_shape=jax.ShapeDtypeStruct((n, Q_PAD), BF16),
        compiler_params=_params(3),
        name="attention",
    )(q, kc, vc, ksel, vsel, kwin, vwin, gates, ovl)


HALO = 8


def _merge_kernel(x_ref, a_ref, cu_ref, halo_ref, bg_ref, g_ref, wm_ref, wa_ref, wc_ref, wo_ref, cw_ref, o_ref,
                  *, tiles_per_seq):
    x = x_ref[...]
    h = _rms_rows(x, g_ref[...]).astype(BF16)
    mg = jax.nn.sigmoid(jnp.dot(h, wm_ref[...], preferred_element_type=F32))
    att = jnp.dot(a_ref[...], wa_ref[...], preferred_element_type=F32)

    cu = cu_ref[...]
    first = pl.program_id(0) % tiles_per_seq == 0
    halo = jnp.where(first, 0.0, halo_ref[...])
    r = lax.broadcasted_iota(jnp.int32, cu.shape, 0)
    prev1 = jnp.where(r == 0, halo[HALO - 1:HALO, :], pltpu.roll(cu, 1, 0))
    prev2 = jnp.where(r == 0, halo[HALO - 2:HALO - 1, :],
                      jnp.where(r == 1, halo[HALO - 1:HALO, :], pltpu.roll(cu, 2, 0)))
    conv = cw_ref[0:1, :] * prev2 + cw_ref[1:2, :] * prev1 + cw_ref[2:3, :] * cu
    c = (bg_ref[...] * conv).astype(BF16)
    cb = jnp.dot(c, wc_ref[...], preferred_element_type=F32)

    merged = (mg[:, :D_MODEL] * att + mg[:, D_MODEL:] * cb).astype(BF16)
    o_ref[...] = x + jnp.dot(merged, wo_ref[...], preferred_element_type=F32)


def _merge(x1, a, cu, bg, g, wm, wa, wc, wo, cw, seq):
    n = x1.shape[0]
    tiles_per_seq = seq // TM
    row = lambda i: (i, 0)
    return pl.pallas_call(
        functools.partial(_merge_kernel, tiles_per_seq=tiles_per_seq),
        grid=(n // TM,),
        in_specs=[
            pl.BlockSpec((TM, D_MODEL), row),
            pl.BlockSpec((TM, Q_PAD), row),
            pl.BlockSpec((TM, CONV_WIDTH), row),
            pl.BlockSpec((HALO, CONV_WIDTH), lambda i: (jnp.maximum(i * (TM // HALO) - 1, 0), 0)),
            pl.BlockSpec((TM, CONV_WIDTH), row),
            _const_spec((1, D_MODEL)),
            _const_spec((D_MODEL, 2 * D_MODEL)),
            _const_spec((Q_PAD, D_MODEL)),
            _const_spec((CONV_WIDTH, D_MODEL)),
            _const_spec((D_MODEL, D_MODEL)),
            _const_spec((CONV_KERNEL, CONV_WIDTH)),
        ],
        out_specs=pl.BlockSpec((TM, D_MODEL), row),
        out_shape=jax.ShapeDtypeStruct((n, D_MODEL), F32),
        compiler_params=_params(1),
        name="merge",
    )(x1, a, cu, cu, bg, g, wm, wa, wc, wo, cw)


def _pad_heads(w, n_heads):
    rows = w.shape[0]
    w = w.reshape(rows, n_heads, HEAD_DIM)
    return jnp.pad(w, ((0, 0), (0, 0), (0, SLAB - HEAD_DIM))).reshape(rows, n_heads * SLAB)


def _pad_lanes(v):
    return jnp.pad(v, ((0, 0), (0, SLAB - v.shape[-1])))


def _rope_tables(seq):
    inv_freq = ROPE_THETA ** (-jnp.arange(0, ROPE_DIM, 2, dtype=F32) / ROPE_DIM)
    ang = jnp.arange(seq, dtype=F32)[:, None] * inv_freq[None, :]
    cos, sin = jnp.cos(ang), jnp.sin(ang)
    zeros = jnp.zeros((seq, SLAB - ROPE_DIM), F32)
    half0 = jnp.zeros((seq, ROPE_HALF), F32)
    cos_t = jnp.concatenate([cos, cos, jnp.ones((seq, SLAB - ROPE_DIM), F32)], axis=1)
    sin_lo = jnp.concatenate([-sin, half0, zeros], axis=1)
    sin_hi = jnp.concatenate([half0, sin, zeros], axis=1)
    return cos_t, sin_lo, sin_hi


def _block_onehot(seq):
    blk = jnp.arange(seq, dtype=jnp.int32) // SEL_BLOCK
    return (jnp.arange(SLAB, dtype=jnp.int32)[None, :] == (blk[:, None] + SEL_LANE0)).astype(F32)


def _overlap_t(seq, n_cmp):
    n = np.arange(n_cmp)
    j = np.arange(seq // SEL_BLOCK)
    cs, ce = n * CMP_STRIDE, n * CMP_STRIDE + CMP_BLOCK
    ss, se = j * SEL_BLOCK, j * SEL_BLOCK + SEL_BLOCK
    real = n < (seq // CMP_STRIDE - CMP_BLOCK // CMP_STRIDE + 1)
    ovl = (cs[None, :] < se[:, None]) & (ce[None, :] > ss[:, None]) & real[None, :]
    return jnp.asarray(ovl.astype(np.float32), dtype=BF16)


def _expand_cmp_w1(w1):
    half = CMP_BLOCK // CMP_STRIDE
    w = w1.reshape(half, CMP_STRIDE, HEAD_DIM, CMP_HIDDEN)
    eye = jnp.eye(N_GROUPS, dtype=w1.dtype)
    big = jnp.einsum('ptdh,ge->ptgdeh', w, eye)
    return big.reshape(half, N_CHUNKS_ROW, N_GROUPS * CMP_HIDDEN).astype(BF16)


def _expand_pe(pe):
    half = CMP_BLOCK // CMP_STRIDE
    p = pe.reshape(half, CMP_STRIDE, 1, HEAD_DIM)
    return jnp.broadcast_to(p, (half, CMP_STRIDE, N_GROUPS, HEAD_DIM)).reshape(half, N_CHUNKS_ROW)


def kernel(x, ffn1_norm_g, ffn1_w_gate, ffn1_w_up, ffn1_w_down, mix_norm_g, w_in, q_norm_g, k_norm_g, cmp_pe_k, cmp_pe_v, cmp_k_w1, cmp_k_w2, cmp_v_w1, cmp_v_w2, conv_w, w_attn_branch, w_conv_branch, w_out, ffn2_norm_g, ffn2_w_gate, ffn2_w_up, ffn2_w_down):
    batch, seq, d_model = x.shape
    depth = ffn1_norm_g.shape[0]
    assert d_model == D_MODEL and seq % TM == 0 and seq % TQ == 0 and seq // SEL_BLOCK + SEL_LANE0 <= SLAB
    n_chunks = seq // CMP_STRIDE
    cos_t, sin_lo, sin_hi = _rope_tables(seq)
    onehot = _block_onehot(seq)
    ovl = _overlap_t(seq, n_chunks)
    xs = x.reshape(batch * seq, D_MODEL)

    for l in range(depth):
        xs = _ffn(xs, ffn1_norm_g[l][None], ffn1_w_gate[l].astype(BF16), ffn1_w_up[l].astype(BF16),
                  ffn1_w_down[l].astype(BF16))

        w = w_in[l]
        c_kv = ATTN_WIDTH
        c_gate = c_kv + N_KV_SLOTS * KV_WIDTH
        c_conv = c_gate + N_GATES
        c_merge = c_conv + 3 * CONV_WIDTH
        kv = [w[:, c_kv + s * KV_WIDTH:c_kv + (s + 1) * KV_WIDTH] for s in range(N_KV_SLOTS)]
        wgate = w[:, c_gate:c_conv].reshape(D_MODEL, 3, N_GROUPS, HEADS_PER_GROUP).transpose(0, 2, 1, 3)
        wgate = jnp.pad(wgate.reshape(D_MODEL, N_GROUPS, 3 * HEADS_PER_GROUP),
                        ((0, 0), (0, 0), (0, SLAB - 3 * HEADS_PER_GROUP))).reshape(D_MODEL, KV_PAD)
        w_b = jnp.concatenate([
            _pad_heads(w[:, :ATTN_WIDTH], N_HEADS),
            _pad_heads(kv[2], N_GROUPS), _pad_heads(kv[4], N_GROUPS),
            _pad_heads(kv[3], N_GROUPS), _pad_heads(kv[5], N_GROUPS),
            kv[0], kv[1], wgate, w[:, c_conv:c_merge]], axis=1).astype(BF16)
        q, ksel, kwin, vsel, vwin, kcmp, vcmp, gates, cu, bg = _inproj(
            xs, mix_norm_g[l][None], w_b, _pad_lanes(q_norm_g[l][None]), _pad_lanes(k_norm_g[l][1:3]),
            cos_t, sin_lo, sin_hi, onehot, seq)

        w2k = _pad_lanes(cmp_k_w2[l]).astype(BF16)
        w2v = _pad_lanes(cmp_v_w2[l]).astype(BF16)
        kc, vc = _compress(kcmp.reshape(batch, n_chunks, N_CHUNKS_ROW), vcmp.reshape(batch, n_chunks, N_CHUNKS_ROW),
                           _expand_pe(cmp_pe_k[l]), _expand_pe(cmp_pe_v[l]),
                           _expand_cmp_w1(cmp_k_w1[l]), _expand_cmp_w1(cmp_v_w1[l]), w2k, w2v,
                           _pad_lanes(k_norm_g[l][0:1]))

        a = _attention(q, kc, vc, ksel, vsel, kwin, vwin, gates, ovl, batch, seq)

        wa = jnp.pad(w_attn_branch[l].reshape(N_HEADS, HEAD_DIM, D_MODEL),
                     ((0, 0), (0, SLAB - HEAD_DIM), (0, 0))).reshape(Q_PAD, D_MODEL).astype(BF16)
        xs = _merge(xs, a, cu, bg, mix_norm_g[l][None], w[:, c_merge:].astype(BF16), wa,
                    w_conv_branch[l].astype(BF16), w_out[l].astype(BF16), conv_w[l], seq)

        xs = _ffn(xs, ffn2_norm_g[l][None], ffn2_w_gate[l].astype(BF16), ffn2_w_up[l].astype(BF16),
                  ffn2_w_down[l].astype(BF16))
    return xs.reshape(batch, seq, D_MODEL)
```

```python
import functools

import numpy as np
import jax
import jax.numpy as jnp
from jax import lax
from jax.experimental import pallas as pl
from jax.experimental.pallas import tpu as pltpu

D_MODEL = 1024
N_HEADS = 8
N_GROUPS = 2
HEADS_PER_GROUP = N_HEADS // N_GROUPS
HEAD_DIM = 64
ATTN_WIDTH = N_HEADS * HEAD_DIM
KV_WIDTH = N_GROUPS * HEAD_DIM
N_KV_SLOTS = 6
ROPE_DIM = HEAD_DIM // 4
ROPE_HALF = ROPE_DIM // 2
ROPE_THETA = 500000.0
CMP_BLOCK = 32
CMP_STRIDE = 16
CMP_HIDDEN = 4 * HEAD_DIM
SEL_BLOCK = 64
N_SEL_BLOCKS = 16
WINDOW = 512
CONV_WIDTH = 512
CONV_KERNEL = 3
D_FF = 2816
NORM_EPS = 1e-6
MASK_VALUE = -1e30
FORCE_SCORE = 1e4
N_GATES = 3 * N_HEADS

LANES = 128
SLAB = LANES
Q_PAD = N_HEADS * SLAB
KV_PAD = N_GROUPS * SLAB
SEL_BIAS = -float(2 ** 30)
SEL_LANE0 = HEAD_DIM

TM = 512
TQ = 256
FF_CHUNK = 1408
VMEM_LIMIT = 56 * 1024 * 1024

F32 = jnp.float32
BF16 = jnp.bfloat16

_NT = (((1,), (1,)), ((), ()))


def _const_spec(shape):
    return pl.BlockSpec(shape, lambda *_: (0,) * len(shape), pipeline_mode=pl.Buffered(1))


def _params(n_axes):
    return pltpu.CompilerParams(dimension_semantics=("parallel",) * n_axes, vmem_limit_bytes=VMEM_LIMIT)


def _rms_rows(x, g):
    ms = jnp.mean(x * x, axis=-1, keepdims=True)
    return x * lax.rsqrt(ms + NORM_EPS) * g


def _ffn_kernel(x_ref, g_ref, wg_ref, wu_ref, wd_ref, o_ref):
    x = x_ref[...]
    h = _rms_rows(x, g_ref[...]).astype(BF16)
    acc = jnp.zeros(x.shape, F32)
    for c in range(D_FF // FF_CHUNK):
        sl = slice(c * FF_CHUNK, (c + 1) * FF_CHUNK)
        gate = jnp.dot(h, wg_ref[:, sl], preferred_element_type=F32)
        up = jnp.dot(h, wu_ref[:, sl], preferred_element_type=F32)
        act = (gate * jax.nn.sigmoid(gate) * up).astype(BF16)
        acc = acc + jnp.dot(act, wd_ref[sl, :], preferred_element_type=F32)
    o_ref[...] = x + 0.5 * acc


def _ffn(x, g, wg, wu, wd):
    n = x.shape[0]
    return pl.pallas_call(
        _ffn_kernel,
        grid=(n // TM,),
        in_specs=[
            pl.BlockSpec((TM, D_MODEL), lambda i: (i, 0)),
            _const_spec((1, D_MODEL)),
            _const_spec((D_MODEL, D_FF)),
            _const_spec((D_MODEL, D_FF)),
            _const_spec((D_FF, D_MODEL)),
        ],
        out_specs=pl.BlockSpec((TM, D_MODEL), lambda i: (i, 0)),
        out_shape=jax.ShapeDtypeStruct((n, D_MODEL), F32),
        compiler_params=_params(1),
        name="ffn",
    )(x, g, wg, wu, wd)


_C_Q = 0
_C_KSEL = _C_Q + Q_PAD
_C_KWIN = _C_KSEL + KV_PAD
_C_VSEL = _C_KWIN + KV_PAD
_C_VWIN = _C_VSEL + KV_PAD
_C_KCMP = _C_VWIN + KV_PAD
_C_VCMP = _C_KCMP + KV_WIDTH
_C_GATE = _C_VCMP + KV_WIDTH
_C_CONV = _C_GATE + KV_PAD
_C_END = _C_CONV + 3 * CONV_WIDTH


def _norm_rope_slab(x, g, cos, sin_lo, sin_hi):
    ms = jnp.sum(x * x, axis=-1, keepdims=True) * (1.0 / HEAD_DIM)
    y = x * lax.rsqrt(ms + NORM_EPS) * g
    return y * cos + pltpu.roll(y, ROPE_HALF, 1) * sin_hi + pltpu.roll(y, SLAB - ROPE_HALF, 1) * sin_lo


def _inproj_kernel(x_ref, g_ref, w_ref, qg_ref, kg_ref, cos_ref, slo_ref, shi_ref, oh_ref,
                   q_ref, ksel_ref, kwin_ref, vsel_ref, vwin_ref, kcmp_ref, vcmp_ref, gate_ref, cu_ref, bg_ref):
    h = _rms_rows(x_ref[...], g_ref[...]).astype(BF16)
    cos, slo, shi = cos_ref[...], slo_ref[...], shi_ref[...]

    def proj(c0, width):
        return jnp.dot(h, w_ref[:, c0:c0 + width], preferred_element_type=F32)

    q = proj(_C_Q, Q_PAD)
    qg = qg_ref[...]
    for hd in range(N_HEADS):
        sl = slice(hd * SLAB, (hd + 1) * SLAB)
        y = _norm_rope_slab(q[:, sl], qg, cos, slo, shi) * (HEAD_DIM ** -0.5)
        q_ref[:, sl] = y.astype(BF16)

    ksel = proj(_C_KSEL, KV_PAD)
    kwin = proj(_C_KWIN, KV_PAD)
    oh = oh_ref[...]
    for g in range(N_GROUPS):
        sl = slice(g * SLAB, (g + 1) * SLAB)
        ksel_ref[:, sl] = (_norm_rope_slab(ksel[:, sl], kg_ref[0:1, :], cos, slo, shi) + oh).astype(BF16)
        kwin_ref[:, sl] = _norm_rope_slab(kwin[:, sl], kg_ref[1:2, :], cos, slo, shi).astype(BF16)

    ones_row = lax.broadcasted_iota(jnp.int32, (KV_PAD, TM), 0) % SLAB == HEAD_DIM
    vsel_ref[...] = jnp.where(ones_row, 1.0, proj(_C_VSEL, KV_PAD).T).astype(BF16)
    vwin_ref[...] = jnp.where(ones_row, 1.0, proj(_C_VWIN, KV_PAD).T).astype(BF16)
    kcmp_ref[...] = proj(_C_KCMP, KV_WIDTH)
    vcmp_ref[...] = proj(_C_VCMP, KV_WIDTH)
    gate_ref[...] = jax.nn.sigmoid(proj(_C_GATE, KV_PAD))
    conv = proj(_C_CONV, 3 * CONV_WIDTH)
    bg_ref[...] = conv[:, :CONV_WIDTH]
    cu_ref[...] = conv[:, CONV_WIDTH:2 * CONV_WIDTH] * conv[:, 2 * CONV_WIDTH:]


def _inproj(x1, g, w_b, qg, kg, cos, slo, shi, oh, seq):
    n = x1.shape[0]
    tiles_per_seq = seq // TM
    row = lambda i: (i, 0)
    pos = lambda i: (i % tiles_per_seq, 0)
    out_widths = [(Q_PAD, BF16), (KV_PAD, BF16), (KV_PAD, BF16), None, None,
                  (KV_WIDTH, F32), (KV_WIDTH, F32), (KV_PAD, F32), (CONV_WIDTH, F32), (CONV_WIDTH, F32)]
    vt_spec = pl.BlockSpec((None, KV_PAD, TM), lambda i: (i // tiles_per_seq, 0, i % tiles_per_seq))
    vt_shape = jax.ShapeDtypeStruct((n // seq, KV_PAD, seq), BF16)
    return pl.pallas_call(
        _inproj_kernel,
        grid=(n // TM,),
        in_specs=[
            pl.BlockSpec((TM, D_MODEL), row),
            _const_spec((1, D_MODEL)),
            _const_spec((D_MODEL, _C_END)),
            _const_spec((1, SLAB)),
            _const_spec((2, SLAB)),
            pl.BlockSpec((TM, SLAB), pos),
            pl.BlockSpec((TM, SLAB), pos),
            pl.BlockSpec((TM, SLAB), pos),
            pl.BlockSpec((TM, SLAB), pos),
        ],
        out_specs=[vt_spec if o is None else pl.BlockSpec((TM, o[0]), row) for o in out_widths],
        out_shape=[vt_shape if o is None else jax.ShapeDtypeStruct((n, o[0]), o[1]) for o in out_widths],
        compiler_params=_params(1),
        name="inproj",
    )(x1, g, w_b, qg, kg, cos, slo, shi, oh)


N_CHUNKS_ROW = CMP_STRIDE * KV_WIDTH


def _compress_kernel(kc_ref, vc_ref, pek_ref, pev_ref, wk1_ref, wv1_ref, wk2_ref, wv2_ref, kg_ref,
                     ko_ref, vo_ref):
    def branch(r_ref, pe_ref, w1_ref, w2_ref):
        r = r_ref[0]
        top = jnp.dot((r + pe_ref[0:1, :]).astype(BF16), w1_ref[0], preferred_element_type=F32)
        bot = jnp.dot((r + pe_ref[1:2, :]).astype(BF16), w1_ref[1], preferred_element_type=F32)
        hid = top + pltpu.roll(bot, bot.shape[0] - 1, 0)
        hid = (hid * jax.nn.sigmoid(hid)).astype(BF16)
        return [jnp.dot(hid[:, g * CMP_HIDDEN:(g + 1) * CMP_HIDDEN], w2_ref[...], preferred_element_type=F32)
                for g in range(N_GROUPS)]

    k = branch(kc_ref, pek_ref, wk1_ref, wk2_ref)
    v = branch(vc_ref, pev_ref, wv1_ref, wv2_ref)
    for g in range(N_GROUPS):
        ms = jnp.sum(k[g] * k[g], axis=-1, keepdims=True) * (1.0 / HEAD_DIM)
        ko_ref[0, g] = (k[g] * lax.rsqrt(ms + NORM_EPS) * kg_ref[...]).astype(BF16)
        vo_ref[0, g] = v[g].T.astype(BF16)


def _compress(kc, vc, pek, pev, wk1, wv1, wk2, wv2, kg):
    b, n_chunks, _ = kc.shape
    blk = pl.BlockSpec((1, n_chunks, N_CHUNKS_ROW), lambda i: (i, 0, 0))
    out = pl.BlockSpec((1, N_GROUPS, n_chunks, SLAB), lambda i: (i, 0, 0, 0))
    return pl.pallas_call(
        _compress_kernel,
        grid=(b,),
        in_specs=[blk, blk,
                  _const_spec((2, N_CHUNKS_ROW)), _const_spec((2, N_CHUNKS_ROW)),
                  _const_spec((2, N_CHUNKS_ROW, N_GROUPS * CMP_HIDDEN)),
                  _const_spec((2, N_CHUNKS_ROW, N_GROUPS * CMP_HIDDEN)),
                  _const_spec((CMP_HIDDEN, SLAB)), _const_spec((CMP_HIDDEN, SLAB)),
                  _const_spec((1, SLAB))],
        out_specs=[out, out],
        out_shape=[jax.ShapeDtypeStruct((b, N_GROUPS, n_chunks, SLAB), BF16)] * 2,
        compiler_params=_params(1),
        name="compress",
    )(kc, vc, pek, pev, wk1, wv1, wk2, wv2, kg)


N_QL = HEADS_PER_GROUP * TQ


def _attn_kernel(q_ref, kc_ref, vc_ref, ksel_ref, vsel_ref, kwin_ref, vwin_ref, gate_ref, ovl_ref, o_ref,
                 m_ref, acc_ref, *, n_sel_blocks, k_eff):
    qt = pl.program_id(2)
    q0 = qt * TQ
    n_cmp = kc_ref.shape[0]

    def heads(x):
        return jnp.concatenate([x] * HEADS_PER_GROUP, axis=1)

    q_t = [q_ref[:, h * SLAB:(h + 1) * SLAB].astype(F32).T[:HEAD_DIM] for h in range(HEADS_PER_GROUP)]
    pad = jnp.zeros((SLAB - HEAD_DIM, TQ), F32)
    q_all = jnp.concatenate([jnp.concatenate([q_t[h], pad], axis=0) for h in range(HEADS_PER_GROUP)],
                            axis=1).astype(BF16)

    pos_c = q0 + lax.broadcasted_iota(jnp.int32, (n_cmp, TQ), 1)
    cmp_end = lax.broadcasted_iota(jnp.int32, (n_cmp, TQ), 0) * CMP_STRIDE + (CMP_BLOCK - 1)
    valid_c = heads(cmp_end <= pos_c)
    s = jnp.dot(kc_ref[...], q_all, preferred_element_type=F32)
    s = jnp.where(valid_c, s, MASK_VALUE)
    e = jnp.exp(s - jnp.max(s, axis=0, keepdims=True))
    p = jnp.where(valid_c, e * (1.0 / jnp.sum(e, axis=0, keepdims=True)), 0.0)
    o_cmp = jnp.dot(vc_ref[...], p.astype(BF16), preferred_element_type=F32)
    p_sum = p[:, :TQ]
    for h in range(1, HEADS_PER_GROUP):
        p_sum = p_sum + p[:, h * TQ:(h + 1) * TQ]

    p_hi = p_sum.astype(BF16)
    p_lo = (p_sum - p_hi.astype(F32)).astype(BF16)
    ovl = ovl_ref[...]
    imp = (jnp.dot(ovl, p_hi, preferred_element_type=F32)
           + jnp.dot(ovl, p_lo, preferred_element_type=F32))
    j_idx = lax.broadcasted_iota(jnp.int32, (n_sel_blocks, TQ), 0)
    pos_t = q0 + lax.broadcasted_iota(jnp.int32, (n_sel_blocks, TQ), 1)
    cur = pos_t // SEL_BLOCK
    imp = jnp.where(j_idx * SEL_BLOCK > pos_t, -1.0, imp)
    imp = jnp.where(j_idx == 0, FORCE_SCORE, imp)
    imp = jnp.where(j_idx == cur, FORCE_SCORE, imp)
    imp = jnp.where(j_idx == cur - 1, FORCE_SCORE, imp)
    rank = jnp.zeros((n_sel_blocks, TQ), F32)
    for jp in range(n_sel_blocks):
        row = imp[jp:jp + 1, :]
        tie = jnp.where(j_idx > jp, 1.0, 0.0)
        rank = rank + jnp.where(row > imp, 1.0, jnp.where(row == imp, tie, 0.0))
    bias = jnp.where(rank < k_eff, 0.0, SEL_BIAS)
    pad = jnp.zeros((SLAB - HEAD_DIM - n_sel_blocks, TQ), F32)
    qa_all = jnp.concatenate([jnp.concatenate([q_t[h], bias, pad], axis=0) for h in range(HEADS_PER_GROUP)],
                             axis=1).astype(BF16)

    key_r = lax.broadcasted_iota(jnp.int32, (TQ, TQ), 0)
    qry_c = lax.broadcasted_iota(jnp.int32, (TQ, TQ), 1)
    diff0 = heads(qry_c - key_r)

    def step(s, v_t):
        m_old = m_ref[...]
        m_new = jnp.maximum(m_old, jnp.max(s, axis=0, keepdims=True))
        p = jnp.exp(s - m_new).astype(BF16)
        acc_ref[...] = jnp.exp(m_old - m_new) * acc_ref[...] + jnp.dot(v_t, p, preferred_element_type=F32)
        m_ref[...] = m_new

    def reset():
        m_ref[...] = jnp.full(m_ref.shape, MASK_VALUE, F32)
        acc_ref[...] = jnp.zeros(acc_ref.shape, F32)

    def result():
        acc = acc_ref[...]
        return acc * (1.0 / acc[HEAD_DIM:HEAD_DIM + 1, :])

    reset()

    @pl.loop(0, qt)
    def _(kt):
        k0 = pl.multiple_of(kt * TQ, TQ)
        s = jnp.dot(ksel_ref[pl.ds(k0, TQ), :], qa_all, preferred_element_type=F32)
        step(s, vsel_ref[:, pl.ds(k0, TQ)])

    k0 = pl.multiple_of(q0, TQ)
    s = jnp.dot(ksel_ref[pl.ds(k0, TQ), :], qa_all, preferred_element_type=F32)
    step(jnp.where(diff0 >= 0, s, MASK_VALUE), vsel_ref[:, pl.ds(k0, TQ)])
    o_sel = result()

    reset()
    for back in range(WINDOW // TQ, -1, -1):
        kt_raw = qt - back
        k0 = pl.multiple_of(jnp.maximum(kt_raw, 0) * TQ, TQ)
        s = jnp.dot(kwin_ref[pl.ds(k0, TQ), :], q_all, preferred_element_type=F32)
        diff = diff0 + back * TQ
        ok = jnp.where(diff >= 0, jnp.where(diff < WINDOW, 1.0, 0.0), 0.0) * jnp.where(kt_raw >= 0, 1.0, 0.0)
        step(jnp.where(ok > 0.0, s, MASK_VALUE), vwin_ref[:, pl.ds(k0, TQ)])
    o_win = result()

    gates_t = gate_ref[...].T
    real = lax.broadcasted_iota(jnp.int32, (SLAB, TQ), 0) < HEAD_DIM
    for h in range(HEADS_PER_GROUP):
        sl = slice(h * TQ, (h + 1) * TQ)
        o = (gates_t[h:h + 1] * o_cmp[:, sl]
             + gates_t[HEADS_PER_GROUP + h:HEADS_PER_GROUP + h + 1] * o_sel[:, sl]
             + gates_t[2 * HEADS_PER_GROUP + h:2 * HEADS_PER_GROUP + h + 1] * o_win[:, sl])
        o_ref[:, h * SLAB:(h + 1) * SLAB] = jnp.where(real, o, 0.0).T.astype(BF16)


def _attention(q, kc, vc, ksel, vsel, kwin, vwin, gates, ovl, batch, seq):
    n = q.shape[0]
    n_cmp = kc.shape[2]
    n_sel_blocks = seq // SEL_BLOCK
    tiles = seq // TQ
    cmp_spec = pl.BlockSpec((None, None, n_cmp, SLAB), lambda b, g, t: (b, g, 0, 0))
    k_spec = pl.BlockSpec((seq, SLAB), lambda b, g, t: (b, g))
    v_spec = pl.BlockSpec((None, SLAB, seq), lambda b, g, t: (b, g, 0))
    return pl.pallas_call(
        functools.partial(_attn_kernel, n_sel_blocks=n_sel_blocks, k_eff=min(N_SEL_BLOCKS, n_sel_blocks)),
        grid=(batch, N_GROUPS, tiles),
        in_specs=[
            pl.BlockSpec((TQ, HEADS_PER_GROUP * SLAB), lambda b, g, t: (b * tiles + t, g)),
            cmp_spec, cmp_spec, k_spec, v_spec, k_spec, v_spec,
            pl.BlockSpec((TQ, SLAB), lambda b, g, t: (b * tiles + t, g)),
            _const_spec((n_sel_blocks, n_cmp)),
        ],
        out_specs=pl.BlockSpec((TQ, HEADS_PER_GROUP * SLAB), lambda b, g, t: (b * tiles + t, g)),
        out_shape=jax.ShapeDtypeStruct((n, Q_PAD), BF16),
        scratch_shapes=[pltpu.VMEM((1, N_QL), F32), pltpu.VMEM((SLAB, N_QL), F32)],
        compiler_params=_params(3),
        name="attention",
    )(q, kc, vc, ksel, vsel, kwin, vwin, gates, ovl)


HALO = 8


def _merge_kernel(x_ref, a_ref, cu_ref, halo_ref, bg_ref, g_ref, wm_ref, wa_ref, wc_ref, wo_ref, cw_ref, o_ref,
                  *, tiles_per_seq):
    x = x_ref[...]
    h = _rms_rows(x, g_ref[...]).astype(BF16)
    mg = jax.nn.sigmoid(jnp.dot(h, wm_ref[...], preferred_element_type=F32))
    att = jnp.dot(a_ref[...], wa_ref[...], preferred_element_type=F32)

    cu = cu_ref[...]
    first = pl.program_id(0) % tiles_per_seq == 0
    halo = jnp.where(first, 0.0, halo_ref[...])
    r = lax.broadcasted_iota(jnp.int32, cu.shape, 0)
    prev1 = jnp.where(r == 0, halo[HALO - 1:HALO, :], pltpu.roll(cu, 1, 0))
    prev2 = jnp.where(r == 0, halo[HALO - 2:HALO - 1, :],
                      jnp.where(r == 1, halo[HALO - 1:HALO, :], pltpu.roll(cu, 2, 0)))
    conv = cw_ref[0:1, :] * prev2 + cw_ref[1:2, :] * prev1 + cw_ref[2:3, :] * cu
    c = (bg_ref[...] * conv).astype(BF16)
    cb = jnp.dot(c, wc_ref[...], preferred_element_type=F32)

    merged = (mg[:, :D_MODEL] * att + mg[:, D_MODEL:] * cb).astype(BF16)
    o_ref[...] = x + jnp.dot(merged, wo_ref[...], preferred_element_type=F32)


def _merge(x1, a, cu, bg, g, wm, wa, wc, wo, cw, seq):
    n = x1.shape[0]
    tiles_per_seq = seq // TM
    row = lambda i: (i, 0)
    return pl.pallas_call(
        functools.partial(_merge_kernel, tiles_per_seq=tiles_per_seq),
        grid=(n // TM,),
        in_specs=[
            pl.BlockSpec((TM, D_MODEL), row),
            pl.BlockSpec((TM, Q_PAD), row),
            pl.BlockSpec((TM, CONV_WIDTH), row),
            pl.BlockSpec((HALO, CONV_WIDTH), lambda i: (jnp.maximum(i * (TM // HALO) - 1, 0), 0)),
            pl.BlockSpec((TM, CONV_WIDTH), row),
            _const_spec((1, D_MODEL)),
            _const_spec((D_MODEL, 2 * D_MODEL)),
            _const_spec((Q_PAD, D_MODEL)),
            _const_spec((CONV_WIDTH, D_MODEL)),
            _const_spec((D_MODEL, D_MODEL)),
            _const_spec((CONV_KERNEL, CONV_WIDTH)),
        ],
        out_specs=pl.BlockSpec((TM, D_MODEL), row),
        out_shape=jax.ShapeDtypeStruct((n, D_MODEL), F32),
        compiler_params=_params(1),
        name="merge",
    )(x1, a, cu, cu, bg, g, wm, wa, wc, wo, cw)


def _pad_heads(w, n_heads):
    rows = w.shape[0]
    w = w.reshape(rows, n_heads, HEAD_DIM)
    return jnp.pad(w, ((0, 0), (0, 0), (0, SLAB - HEAD_DIM))).reshape(rows, n_heads * SLAB)


def _pad_lanes(v):
    return jnp.pad(v, ((0, 0), (0, SLAB - v.shape[-1])))


def _rope_tables(seq):
    inv_freq = ROPE_THETA ** (-jnp.arange(0, ROPE_DIM, 2, dtype=F32) / ROPE_DIM)
    ang = jnp.arange(seq, dtype=F32)[:, None] * inv_freq[None, :]
    cos, sin = jnp.cos(ang), jnp.sin(ang)
    zeros = jnp.zeros((seq, SLAB - ROPE_DIM), F32)
    half0 = jnp.zeros((seq, ROPE_HALF), F32)
    cos_t = jnp.concatenate([cos, cos, jnp.ones((seq, SLAB - ROPE_DIM), F32)], axis=1)
    sin_lo = jnp.concatenate([-sin, half0, zeros], axis=1)
    sin_hi = jnp.concatenate([half0, sin, zeros], axis=1)
    return cos_t, sin_lo, sin_hi


def _block_onehot(seq):
    blk = jnp.arange(seq, dtype=jnp.int32) // SEL_BLOCK
    return (jnp.arange(SLAB, dtype=jnp.int32)[None, :] == (blk[:, None] + SEL_LANE0)).astype(F32)


def _overlap_t(seq, n_cmp):
    n = np.arange(n_cmp)
    j = np.arange(seq // SEL_BLOCK)
    cs, ce = n * CMP_STRIDE, n * CMP_STRIDE + CMP_BLOCK
    ss, se = j * SEL_BLOCK, j * SEL_BLOCK + SEL_BLOCK
    real = n < (seq // CMP_STRIDE - CMP_BLOCK // CMP_STRIDE + 1)
    ovl = (cs[None, :] < se[:, None]) & (ce[None, :] > ss[:, None]) & real[None, :]
    return jnp.asarray(ovl.astype(np.float32), dtype=BF16)


def _expand_cmp_w1(w1):
    half = CMP_BLOCK // CMP_STRIDE
    w = w1.reshape(half, CMP_STRIDE, HEAD_DIM, CMP_HIDDEN)
    eye = jnp.eye(N_GROUPS, dtype=w1.dtype)
    big = jnp.einsum('ptdh,ge->ptgdeh', w, eye)
    return big.reshape(half, N_CHUNKS_ROW, N_GROUPS * CMP_HIDDEN).astype(BF16)


def _expand_pe(pe):
    half = CMP_BLOCK // CMP_STRIDE
    p = pe.reshape(half, CMP_STRIDE, 1, HEAD_DIM)
    return jnp.broadcast_to(p, (half, CMP_STRIDE, N_GROUPS, HEAD_DIM)).reshape(half, N_CHUNKS_ROW)


def kernel(x, ffn1_norm_g, ffn1_w_gate, ffn1_w_up, ffn1_w_down, mix_norm_g, w_in, q_norm_g, k_norm_g, cmp_pe_k, cmp_pe_v, cmp_k_w1, cmp_k_w2, cmp_v_w1, cmp_v_w2, conv_w, w_attn_branch, w_conv_branch, w_out, ffn2_norm_g, ffn2_w_gate, ffn2_w_up, ffn2_w_down):
    batch, seq, d_model = x.shape
    depth = ffn1_norm_g.shape[0]
    assert d_model == D_MODEL and seq % TM == 0 and seq % TQ == 0 and seq // SEL_BLOCK + SEL_LANE0 <= SLAB
    n_chunks = seq // CMP_STRIDE
    cos_t, sin_lo, sin_hi = _rope_tables(seq)
    onehot = _block_onehot(seq)
    ovl = _overlap_t(seq, n_chunks)
    xs = x.reshape(batch * seq, D_MODEL)

    for l in range(depth):
        xs = _ffn(xs, ffn1_norm_g[l][None], ffn1_w_gate[l].astype(BF16), ffn1_w_up[l].astype(BF16),
                  ffn1_w_down[l].astype(BF16))

        w = w_in[l]
        c_kv = ATTN_WIDTH
        c_gate = c_kv + N_KV_SLOTS * KV_WIDTH
        c_conv = c_gate + N_GATES
        c_merge = c_conv + 3 * CONV_WIDTH
        kv = [w[:, c_kv + s * KV_WIDTH:c_kv + (s + 1) * KV_WIDTH] for s in range(N_KV_SLOTS)]
        wgate = w[:, c_gate:c_conv].reshape(D_MODEL, 3, N_GROUPS, HEADS_PER_GROUP).transpose(0, 2, 1, 3)
        wgate = jnp.pad(wgate.reshape(D_MODEL, N_GROUPS, 3 * HEADS_PER_GROUP),
                        ((0, 0), (0, 0), (0, SLAB - 3 * HEADS_PER_GROUP))).reshape(D_MODEL, KV_PAD)
        w_b = jnp.concatenate([
            _pad_heads(w[:, :ATTN_WIDTH], N_HEADS),
            _pad_heads(kv[2], N_GROUPS), _pad_heads(kv[4], N_GROUPS),
            _pad_heads(kv[3], N_GROUPS), _pad_heads(kv[5], N_GROUPS),
            kv[0], kv[1], wgate, w[:, c_conv:c_merge]], axis=1).astype(BF16)
        q, ksel, kwin, vsel, vwin, kcmp, vcmp, gates, cu, bg = _inproj(
            xs, mix_norm_g[l][None], w_b, _pad_lanes(q_norm_g[l][None]), _pad_lanes(k_norm_g[l][1:3]),
            cos_t, sin_lo, sin_hi, onehot, seq)

        w2k = _pad_lanes(cmp_k_w2[l]).astype(BF16)
        w2v = _pad_lanes(cmp_v_w2[l]).astype(BF16)
        kc, vc = _compress(kcmp.reshape(batch, n_chunks, N_CHUNKS_ROW), vcmp.reshape(batch, n_chunks, N_CHUNKS_ROW),
                           _expand_pe(cmp_pe_k[l]), _expand_pe(cmp_pe_v[l]),
                           _expand_cmp_w1(cmp_k_w1[l]), _expand_cmp_w1(cmp_v_w1[l]), w2k, w2v,
                           _pad_lanes(k_norm_g[l][0:1]))

        a = _attention(q, kc, vc, ksel, vsel, kwin, vwin, gates, ovl, batch, seq)

        wa = jnp.pad(w_attn_branch[l].reshape(N_HEADS, HEAD_DIM, D_MODEL),
                     ((0, 0), (0, SLAB - HEAD_DIM), (0, 0))).reshape(Q_PAD, D_MODEL).astype(BF16)
        xs = _merge(xs, a, cu, bg, mix_norm_g[l][None], w[:, c_merge:].astype(BF16), wa,
                    w_conv_branch[l].astype(BF16), w_out[l].astype(BF16), conv_w[l], seq)

        xs = _ffn(xs, ffn2_norm_g[l][None], ffn2_w_gate[l].astype(BF16), ffn2_w_up[l].astype(BF16),
                  ffn2_w_down[l].astype(BF16))
    return xs.reshape(batch, seq, D_MODEL)
```

```python
import functools
import math

import numpy as np
import jax
import jax.numpy as jnp
from jax import lax
from jax.experimental import pallas as pl
from jax.experimental.pallas import tpu as pltpu

D_MODEL = 1024
N_HEADS = 8
N_GROUPS = 2
HEADS_PER_GROUP = N_HEADS // N_GROUPS
HEAD_DIM = 64
ATTN_WIDTH = N_HEADS * HEAD_DIM
KV_WIDTH = N_GROUPS * HEAD_DIM
N_KV_SLOTS = 6
ROPE_DIM = HEAD_DIM // 4
ROPE_HALF = ROPE_DIM // 2
ROPE_THETA = 500000.0
CMP_BLOCK = 32
CMP_STRIDE = 16
CMP_HIDDEN = 4 * HEAD_DIM
SEL_BLOCK = 64
N_SEL_BLOCKS = 16
WINDOW = 512
CONV_WIDTH = 512
CONV_KERNEL = 3
D_FF = 2816
NORM_EPS = 1e-6
MASK_VALUE = -1e30
FORCE_SCORE = 1e4
N_GATES = 3 * N_HEADS

LANES = 128
SLAB = LANES
MXU_DIM = 256
Q_PAD = N_HEADS * SLAB
KV_PAD = N_GROUPS * SLAB
V_ROWS = 80
VT_PAD = N_GROUPS * V_ROWS
SEL_BIAS = -float(2 ** 30)
SEL_LANE0 = HEAD_DIM
Q_SCALE = HEAD_DIM ** -0.5 * math.log2(math.e)

TM = 512
TQ = 256
FF_CHUNKS = (6 * MXU_DIM, 5 * MXU_DIM)
VMEM_LIMIT = 56 * 1024 * 1024

F32 = jnp.float32
BF16 = jnp.bfloat16


def _const_spec(shape):
    return pl.BlockSpec(shape, lambda *_: (0,) * len(shape), pipeline_mode=pl.Buffered(1))


def _params(n_axes):
    return pltpu.CompilerParams(dimension_semantics=("parallel",) * n_axes, vmem_limit_bytes=VMEM_LIMIT)


def _rms_rows(x, g):
    ms = jnp.mean(x * x, axis=-1, keepdims=True)
    return x * lax.rsqrt(ms + NORM_EPS) * g


def _ffn_kernel(x_ref, g_ref, wg_ref, wu_ref, wd_ref, o_ref):
    x = x_ref[...]
    h = _rms_rows(x, g_ref[...]).astype(BF16)
    acc = jnp.zeros(x.shape, F32)
    c0 = 0
    for width in FF_CHUNKS:
        sl = slice(c0, c0 + width)
        gate = jnp.dot(h, wg_ref[:, sl], preferred_element_type=F32)
        up = jnp.dot(h, wu_ref[:, sl], preferred_element_type=F32)
        act = (gate * jax.nn.sigmoid(gate) * up).astype(BF16)
        acc = acc + jnp.dot(act, wd_ref[sl, :], preferred_element_type=F32)
        c0 += width
    o_ref[...] = x + 0.5 * acc


def _ffn(x, g, wg, wu, wd):
    n = x.shape[0]
    return pl.pallas_call(
        _ffn_kernel,
        grid=(n // TM,),
        in_specs=[
            pl.BlockSpec((TM, D_MODEL), lambda i: (i, 0)),
            _const_spec((1, D_MODEL)),
            _const_spec((D_MODEL, D_FF)),
            _const_spec((D_MODEL, D_FF)),
            _const_spec((D_FF, D_MODEL)),
        ],
        out_specs=pl.BlockSpec((TM, D_MODEL), lambda i: (i, 0)),
        out_shape=jax.ShapeDtypeStruct((n, D_MODEL), F32),
        compiler_params=_params(1),
        name="ffn",
    )(x, g, wg, wu, wd)


_C_Q = 0
_C_KSEL = _C_Q + Q_PAD
_C_KWIN = _C_KSEL + KV_PAD
_C_VSEL = _C_KWIN + KV_PAD
_C_VWIN = _C_VSEL + KV_PAD
_C_KCMP = _C_VWIN + KV_PAD
_C_VCMP = _C_KCMP + KV_WIDTH
_C_GATE = _C_VCMP + KV_WIDTH
_C_CONV = _C_GATE + KV_PAD
_C_END = _C_CONV + 3 * CONV_WIDTH


def _norm_rope_slab(x, g, cos, sin_lo, sin_hi):
    ms = jnp.sum(x * x, axis=-1, keepdims=True) * (1.0 / HEAD_DIM)
    y = x * lax.rsqrt(ms + NORM_EPS) * g
    return y * cos + pltpu.roll(y, ROPE_HALF, 1) * sin_hi + pltpu.roll(y, SLAB - ROPE_HALF, 1) * sin_lo


def _inproj_kernel(x_ref, g_ref, w_ref, qg_ref, kg_ref, cos_ref, slo_ref, shi_ref, oh_ref,
                   q_ref, ksel_ref, kwin_ref, vsel_ref, vwin_ref, kcmp_ref, vcmp_ref, gate_ref, cu_ref, bg_ref):
    h = _rms_rows(x_ref[...], g_ref[...]).astype(BF16)
    cos, slo, shi = cos_ref[...], slo_ref[...], shi_ref[...]

    def proj(c0, width):
        return jnp.dot(h, w_ref[:, c0:c0 + width], preferred_element_type=F32)

    q = proj(_C_Q, Q_PAD)
    qg = qg_ref[...]
    for hd in range(N_HEADS):
        sl = slice(hd * SLAB, (hd + 1) * SLAB)
        y = _norm_rope_slab(q[:, sl], qg, cos, slo, shi) * Q_SCALE
        q_ref[:, sl] = y.astype(BF16)

    ksel = proj(_C_KSEL, KV_PAD)
    kwin = proj(_C_KWIN, KV_PAD)
    oh = oh_ref[...]
    for g in range(N_GROUPS):
        sl = slice(g * SLAB, (g + 1) * SLAB)
        ksel_ref[:, sl] = (_norm_rope_slab(ksel[:, sl], kg_ref[0:1, :], cos, slo, shi) + oh).astype(BF16)
        kwin_ref[:, sl] = _norm_rope_slab(kwin[:, sl], kg_ref[1:2, :], cos, slo, shi).astype(BF16)

    row = lax.broadcasted_iota(jnp.int32, (V_ROWS, TM), 0)
    for v_ref, c0 in ((vsel_ref, _C_VSEL), (vwin_ref, _C_VWIN)):
        v_t = proj(c0, KV_PAD).T
        for g in range(N_GROUPS):
            v_g = v_t[g * SLAB:g * SLAB + V_ROWS]
            v_ref[g * V_ROWS:(g + 1) * V_ROWS, :] = jnp.where(row == HEAD_DIM, 1.0, v_g).astype(BF16)
    kcmp_ref[...] = proj(_C_KCMP, KV_WIDTH)
    vcmp_ref[...] = proj(_C_VCMP, KV_WIDTH)
    gate_ref[...] = jax.nn.sigmoid(proj(_C_GATE, KV_PAD))
    conv = proj(_C_CONV, 3 * CONV_WIDTH)
    bg_ref[...] = conv[:, :CONV_WIDTH]
    cu_ref[...] = conv[:, CONV_WIDTH:2 * CONV_WIDTH] * conv[:, 2 * CONV_WIDTH:]


def _inproj(x1, g, w_b, qg, kg, cos, slo, shi, oh, seq):
    n = x1.shape[0]
    tiles_per_seq = seq // TM
    row = lambda i: (i, 0)
    pos = lambda i: (i % tiles_per_seq, 0)
    out_widths = [(Q_PAD, BF16), (KV_PAD, BF16), (KV_PAD, BF16), None, None,
                  (KV_WIDTH, F32), (KV_WIDTH, F32), (KV_PAD, F32), (CONV_WIDTH, F32), (CONV_WIDTH, F32)]
    vt_spec = pl.BlockSpec((None, VT_PAD, TM), lambda i: (i // tiles_per_seq, 0, i % tiles_per_seq))
    vt_shape = jax.ShapeDtypeStruct((n // seq, VT_PAD, seq), BF16)
    return pl.pallas_call(
        _inproj_kernel,
        grid=(n // TM,),
        in_specs=[
            pl.BlockSpec((TM, D_MODEL), row),
            _const_spec((1, D_MODEL)),
            _const_spec((D_MODEL, _C_END)),
            _const_spec((1, SLAB)),
            _const_spec((2, SLAB)),
            pl.BlockSpec((TM, SLAB), pos),
            pl.BlockSpec((TM, SLAB), pos),
            pl.BlockSpec((TM, SLAB), pos),
            pl.BlockSpec((TM, SLAB), pos),
        ],
        out_specs=[vt_spec if o is None else pl.BlockSpec((TM, o[0]), row) for o in out_widths],
        out_shape=[vt_shape if o is None else jax.ShapeDtypeStruct((n, o[0]), o[1]) for o in out_widths],
        compiler_params=_params(1),
        name="inproj",
    )(x1, g, w_b, qg, kg, cos, slo, shi, oh)


N_CHUNKS_ROW = CMP_STRIDE * KV_WIDTH


def _compress_kernel(kc_ref, vc_ref, pek_ref, pev_ref, wk1_ref, wv1_ref, wk2_ref, wv2_ref, kg_ref,
                     ko_ref, vo_ref):
    def branch(r_ref, pe_ref, w1_ref, w2_ref):
        r = r_ref[0]
        top = jnp.dot((r + pe_ref[0:1, :]).astype(BF16), w1_ref[0], preferred_element_type=F32)
        bot = jnp.dot((r + pe_ref[1:2, :]).astype(BF16), w1_ref[1], preferred_element_type=F32)
        hid = top + pltpu.roll(bot, bot.shape[0] - 1, 0)
        hid = (hid * jax.nn.sigmoid(hid)).astype(BF16)
        return [jnp.dot(hid[:, g * CMP_HIDDEN:(g + 1) * CMP_HIDDEN], w2_ref[...], preferred_element_type=F32)
                for g in range(N_GROUPS)]

    k = branch(kc_ref, pek_ref, wk1_ref, wk2_ref)
    v = branch(vc_ref, pev_ref, wv1_ref, wv2_ref)
    for g in range(N_GROUPS):
        ms = jnp.sum(k[g] * k[g], axis=-1, keepdims=True) * (1.0 / HEAD_DIM)
        ko_ref[0, g] = (k[g] * lax.rsqrt(ms + NORM_EPS) * kg_ref[...]).astype(BF16)
        vo_ref[0, g] = v[g].T.astype(BF16)


def _compress(kc, vc, pek, pev, wk1, wv1, wk2, wv2, kg):
    b, n_chunks, _ = kc.shape
    blk = pl.BlockSpec((1, n_chunks, N_CHUNKS_ROW), lambda i: (i, 0, 0))
    k_out = pl.BlockSpec((1, N_GROUPS, n_chunks, SLAB), lambda i: (i, 0, 0, 0))
    v_out = pl.BlockSpec((1, N_GROUPS, SLAB, n_chunks), lambda i: (i, 0, 0, 0))
    return pl.pallas_call(
        _compress_kernel,
        grid=(b,),
        in_specs=[blk, blk,
                  _const_spec((2, N_CHUNKS_ROW)), _const_spec((2, N_CHUNKS_ROW)),
                  _const_spec((2, N_CHUNKS_ROW, N_GROUPS * CMP_HIDDEN)),
                  _const_spec((2, N_CHUNKS_ROW, N_GROUPS * CMP_HIDDEN)),
                  _const_spec((CMP_HIDDEN, SLAB)), _const_spec((CMP_HIDDEN, SLAB)),
                  _const_spec((1, SLAB))],
        out_specs=[k_out, v_out],
        out_shape=[jax.ShapeDtypeStruct((b, N_GROUPS, n_chunks, SLAB), BF16),
                   jax.ShapeDtypeStruct((b, N_GROUPS, SLAB, n_chunks), BF16)],
        compiler_params=_params(1),
        name="compress",
    )(kc, vc, pek, pev, wk1, wv1, wk2, wv2, kg)


N_QL = HEADS_PER_GROUP * TQ


def _attn_kernel(q_ref, kc_ref, vc_ref, ksel_ref, vsel_ref, kwin_ref, vwin_ref, gate_ref, ovl_ref, o_ref,
                 qa_ref, m_ref, acc_ref, mw_ref, accw_ref, *, n_tiles, n_sel_blocks, k_eff):
    t = pl.program_id(2)
    tile = (t, n_tiles - 1 - t)
    n_cmp = kc_ref.shape[0]

    def per_head(x):
        return jnp.concatenate([x] * HEADS_PER_GROUP, axis=1)

    def rows_of(x):
        return pl.ds(pl.multiple_of(tile[x] * TQ, TQ), TQ)

    q_t = [[q_ref[rows_of(x), h * SLAB:(h + 1) * SLAB].astype(F32).T[:HEAD_DIM] for h in range(HEADS_PER_GROUP)]
           for x in range(2)]
    pad = jnp.zeros((SLAB - HEAD_DIM, TQ), F32)
    q_all = [jnp.concatenate([jnp.concatenate([q_t[x][h], pad], axis=0) for h in range(HEADS_PER_GROUP)],
                             axis=1).astype(BF16) for x in range(2)]
    q_pair = jnp.concatenate(q_all, axis=1)

    lane = lax.broadcasted_iota(jnp.int32, (n_cmp, TQ), 1)
    cmp_end = lax.broadcasted_iota(jnp.int32, (n_cmp, TQ), 0) * CMP_STRIDE + (CMP_BLOCK - 1)
    valid_c = jnp.concatenate([per_head(cmp_end <= tile[x] * TQ + lane) for x in range(2)], axis=1)
    s = jnp.dot(kc_ref[...], q_pair, preferred_element_type=F32)
    s = jnp.where(valid_c, s, MASK_VALUE)
    e = jnp.exp2(s - jnp.max(s, axis=0, keepdims=True))
    p = jnp.where(valid_c, e * (1.0 / jnp.sum(e, axis=0, keepdims=True)), 0.0)
    o_cmp = jnp.dot(vc_ref[...], p.astype(BF16), preferred_element_type=F32)
    p_sum = []
    for x in range(2):
        acc = p[:, x * N_QL:x * N_QL + TQ]
        for h in range(1, HEADS_PER_GROUP):
            acc = acc + p[:, x * N_QL + h * TQ:x * N_QL + (h + 1) * TQ]
        p_sum.append(acc)
    p_sum = jnp.concatenate(p_sum, axis=1)

    p_hi = p_sum.astype(BF16)
    p_lo = (p_sum - p_hi.astype(F32)).astype(BF16)
    ovl = ovl_ref[...]
    imp = (jnp.dot(ovl, p_hi, preferred_element_type=F32)
           + jnp.dot(ovl, p_lo, preferred_element_type=F32))
    j_idx = lax.broadcasted_iota(jnp.int32, (n_sel_blocks, 2 * TQ), 0)
    lane2 = lax.broadcasted_iota(jnp.int32, (n_sel_blocks, 2 * TQ), 1)
    pos_t = jnp.where(lane2 < TQ, tile[0] * TQ + lane2, tile[1] * TQ + lane2 - TQ)
    cur = pos_t // SEL_BLOCK
    imp = jnp.where(j_idx * SEL_BLOCK > pos_t, -1.0, imp)
    imp = jnp.where(j_idx == 0, FORCE_SCORE, imp)
    imp = jnp.where(j_idx == cur, FORCE_SCORE, imp)
    imp = jnp.where(j_idx == cur - 1, FORCE_SCORE, imp)
    rank = jnp.zeros((n_sel_blocks, 2 * TQ), F32)
    for jp in range(n_sel_blocks):
        row = imp[jp:jp + 1, :]
        tie = jnp.where(j_idx > jp, 1.0, 0.0)
        rank = rank + jnp.where(row > imp, 1.0, jnp.where(row == imp, tie, 0.0))
    bias = jnp.where(rank < k_eff, 0.0, SEL_BIAS)
    pad = jnp.zeros((SLAB - HEAD_DIM - n_sel_blocks, TQ), F32)
    for x in range(2):
        bias_x = bias[:, x * TQ:(x + 1) * TQ]
        qa_ref[x] = jnp.concatenate([jnp.concatenate([q_t[x][h], bias_x, pad], axis=0)
                                     for h in range(HEADS_PER_GROUP)], axis=1).astype(BF16)

    key_r = lax.broadcasted_iota(jnp.int32, (TQ, TQ), 0)
    qry_c = lax.broadcasted_iota(jnp.int32, (TQ, TQ), 1)
    diff0 = per_head(qry_c - key_r)

    def step(mr, ar, x, s, v_t):
        for h in range(HEADS_PER_GROUP):
            sl = slice(h * TQ, (h + 1) * TQ)
            m_old = mr[x, :, sl]
            m_new = jnp.maximum(m_old, jnp.max(s[:, sl], axis=0, keepdims=True))
            p = jnp.exp2(s[:, sl] - m_new).astype(BF16)
            ar[x, :, sl] = (jnp.exp2(m_old - m_new) * ar[x, :, sl]
                            + jnp.dot(v_t, p, preferred_element_type=F32))
            mr[x, :, sl] = m_new

    def result(ar, x):
        acc = ar[x]
        return acc * (1.0 / acc[HEAD_DIM:HEAD_DIM + 1, :])

    def keys(kt):
        return pl.ds(pl.multiple_of(kt * TQ, TQ), TQ)

    for mr, ar in ((m_ref, acc_ref), (mw_ref, accw_ref)):
        mr[...] = jnp.full(mr.shape, MASK_VALUE, F32)
        ar[...] = jnp.zeros(ar.shape, F32)

    sel_work = []
    for i in range(n_tiles - 1):
        x = jnp.where(i >= t, 1, 0)
        kt = jnp.where(i >= t, i - t, i)
        sel_work.append((
            functools.partial(lambda x, kt: jnp.dot(ksel_ref[keys(kt), :], qa_ref[x], preferred_element_type=F32),
                              x, kt),
            functools.partial(lambda x, kt, s: step(m_ref, acc_ref, x, s, vsel_ref[:, keys(kt)]), x, kt)))
    for x in range(2):
        sel_work.append((
            functools.partial(lambda x: jnp.where(
                diff0 >= 0, jnp.dot(ksel_ref[keys(tile[x]), :], qa_ref[x], preferred_element_type=F32), MASK_VALUE),
                x),
            functools.partial(lambda x, s: step(m_ref, acc_ref, x, s, vsel_ref[:, keys(tile[x])]), x)))

    def win_scores(x, back):
        kt_raw = tile[x] - back
        s = jnp.dot(kwin_ref[keys(jnp.maximum(kt_raw, 0)), :], q_all[x], preferred_element_type=F32)
        if back == 0:
            s = jnp.where(diff0 >= 0, s, MASK_VALUE)
        elif back * TQ == WINDOW:
            s = jnp.where(diff0 < 0, s, MASK_VALUE)
        if x == 0 and back > 0:
            s = jnp.where(kt_raw >= 0, s, MASK_VALUE)
        return s

    win_work = []
    for x in range(2):
        for back in range(WINDOW // TQ, -1, -1):
            win_work.append((
                functools.partial(win_scores, x, back),
                functools.partial(lambda x, back, s: step(mw_ref, accw_ref, x, s,
                                                          vwin_ref[:, keys(jnp.maximum(tile[x] - back, 0))]),
                                  x, back)))

    work = []
    for i in range(max(len(sel_work), len(win_work))):
        work += sel_work[i:i + 1] + win_work[i:i + 1]
    ahead = 2
    pending = [w[0]() for w in work[:ahead]]
    for i, (_, update) in enumerate(work):
        if i + ahead < len(work):
            pending.append(work[i + ahead][0]())
        update(pending.pop(0))

    for x in range(2):
        gates_t = gate_ref[rows_of(x), :].T
        o_sel = result(acc_ref, x)
        o_win = result(accw_ref, x)
        outs = []
        for h in range(HEADS_PER_GROUP):
            sl = slice(h * TQ, (h + 1) * TQ)
            o = (gates_t[h:h + 1] * o_cmp[:HEAD_DIM, x * N_QL + h * TQ:x * N_QL + (h + 1) * TQ]
                 + gates_t[HEADS_PER_GROUP + h:HEADS_PER_GROUP + h + 1] * o_sel[:HEAD_DIM, sl]
                 + gates_t[2 * HEADS_PER_GROUP + h:2 * HEADS_PER_GROUP + h + 1] * o_win[:HEAD_DIM, sl])
            outs.append(o)
        for hp in range(HEADS_PER_GROUP // 2):
            pair = jnp.concatenate(outs[2 * hp:2 * hp + 2], axis=0).T
            o_ref[rows_of(x), hp * SLAB:(hp + 1) * SLAB] = pair.astype(BF16)


def _attention(q, kc, vc, ksel, vsel, kwin, vwin, gates, ovl, batch, seq):
    n = q.shape[0]
    n_cmp = kc.shape[2]
    n_sel_blocks = seq // SEL_BLOCK
    n_tiles = seq // TQ
    width = HEADS_PER_GROUP * HEAD_DIM
    bg = lambda b, g, t: (b, g)
    return pl.pallas_call(
        functools.partial(_attn_kernel, n_tiles=n_tiles, n_sel_blocks=n_sel_blocks,
                          k_eff=min(N_SEL_BLOCKS, n_sel_blocks)),
        grid=(batch, N_GROUPS, n_tiles // 2),
        in_specs=[
            pl.BlockSpec((seq, HEADS_PER_GROUP * SLAB), bg),
            pl.BlockSpec((None, None, n_cmp, SLAB), lambda b, g, t: (b, g, 0, 0)),
            pl.BlockSpec((None, None, SLAB, n_cmp), lambda b, g, t: (b, g, 0, 0)),
            pl.BlockSpec((seq, SLAB), bg),
            pl.BlockSpec((None, V_ROWS, seq), lambda b, g, t: (b, g, 0)),
            pl.BlockSpec((seq, SLAB), bg),
            pl.BlockSpec((None, V_ROWS, seq), lambda b, g, t: (b, g, 0)),
            pl.BlockSpec((seq, SLAB), bg),
            _const_spec((n_sel_blocks, n_cmp)),
        ],
        out_specs=pl.BlockSpec((seq, width), bg),
        out_shape=jax.ShapeDtypeStruct((n, N_GROUPS * width), BF16),
        scratch_shapes=[pltpu.VMEM((2, SLAB, N_QL), BF16),
                        pltpu.VMEM((2, 1, N_QL), F32), pltpu.VMEM((2, V_ROWS, N_QL), F32),
                        pltpu.VMEM((2, 1, N_QL), F32), pltpu.VMEM((2, V_ROWS, N_QL), F32)],
        compiler_params=_params(3),
        name="attention",
    )(q, kc, vc, ksel, vsel, kwin, vwin, gates, ovl)


HALO = 8


def _merge_kernel(x_ref, a_ref, cu_ref, halo_ref, bg_ref, g_ref, wm_ref, wa_ref, wc_ref, wo_ref, cw_ref, o_ref,
                  *, tiles_per_seq):
    x = x_ref[...]
    h = _rms_rows(x, g_ref[...]).astype(BF16)
    mg = jax.nn.sigmoid(jnp.dot(h, wm_ref[...], preferred_element_type=F32))
    att = jnp.dot(a_ref[...], wa_ref[...], preferred_element_type=F32)

    cu = cu_ref[...]
    first = pl.program_id(0) % tiles_per_seq == 0
    halo = jnp.where(first, 0.0, halo_ref[...])
    r = lax.broadcasted_iota(jnp.int32, cu.shape, 0)
    prev1 = jnp.where(r == 0, halo[HALO - 1:HALO, :], pltpu.roll(cu, 1, 0))
    prev2 = jnp.where(r == 0, halo[HALO - 2:HALO - 1, :],
                      jnp.where(r == 1, halo[HALO - 1:HALO, :], pltpu.roll(cu, 2, 0)))
    conv = cw_ref[0:1, :] * prev2 + cw_ref[1:2, :] * prev1 + cw_ref[2:3, :] * cu
    c = (bg_ref[...] * conv).astype(BF16)
    cb = jnp.dot(c, wc_ref[...], preferred_element_type=F32)

    merged = (mg[:, :D_MODEL] * att + mg[:, D_MODEL:] * cb).astype(BF16)
    o_ref[...] = x + jnp.dot(merged, wo_ref[...], preferred_element_type=F32)


def _merge(x1, a, cu, bg, g, wm, wa, wc, wo, cw, seq):
    n = x1.shape[0]
    tiles_per_seq = seq // TM
    row = lambda i: (i, 0)
    return pl.pallas_call(
        functools.partial(_merge_kernel, tiles_per_seq=tiles_per_seq),
        grid=(n // TM,),
        in_specs=[
            pl.BlockSpec((TM, D_MODEL), row),
            pl.BlockSpec((TM, ATTN_WIDTH), row),
            pl.BlockSpec((TM, CONV_WIDTH), row),
            pl.BlockSpec((HALO, CONV_WIDTH), lambda i: (jnp.maximum(i * (TM // HALO) - 1, 0), 0)),
            pl.BlockSpec((TM, CONV_WIDTH), row),
            _const_spec((1, D_MODEL)),
            _const_spec((D_MODEL, 2 * D_MODEL)),
            _const_spec((ATTN_WIDTH, D_MODEL)),
            _const_spec((CONV_WIDTH, D_MODEL)),
            _const_spec((D_MODEL, D_MODEL)),
            _const_spec((CONV_KERNEL, CONV_WIDTH)),
        ],
        out_specs=pl.BlockSpec((TM, D_MODEL), row),
        out_shape=jax.ShapeDtypeStruct((n, D_MODEL), F32),
        compiler_params=_params(1),
        name="merge",
    )(x1, a, cu, cu, bg, g, wm, wa, wc, wo, cw)


def _pad_heads(w, n_heads):
    rows = w.shape[0]
    w = w.reshape(rows, n_heads, HEAD_DIM)
    return jnp.pad(w, ((0, 0), (0, 0), (0, SLAB - HEAD_DIM))).reshape(rows, n_heads * SLAB)


def _pad_lanes(v):
    return jnp.pad(v, ((0, 0), (0, SLAB - v.shape[-1])))


def _rope_tables(seq):
    inv_freq = ROPE_THETA ** (-jnp.arange(0, ROPE_DIM, 2, dtype=F32) / ROPE_DIM)
    ang = jnp.arange(seq, dtype=F32)[:, None] * inv_freq[None, :]
    cos, sin = jnp.cos(ang), jnp.sin(ang)
    zeros = jnp.zeros((seq, SLAB - ROPE_DIM), F32)
    half0 = jnp.zeros((seq, ROPE_HALF), F32)
    cos_t = jnp.concatenate([cos, cos, jnp.ones((seq, SLAB - ROPE_DIM), F32)], axis=1)
    sin_lo = jnp.concatenate([-sin, half0, zeros], axis=1)
    sin_hi = jnp.concatenate([half0, sin, zeros], axis=1)
    return cos_t, sin_lo, sin_hi


def _block_onehot(seq):
    blk = jnp.arange(seq, dtype=jnp.int32) // SEL_BLOCK
    return (jnp.arange(SLAB, dtype=jnp.int32)[None, :] == (blk[:, None] + SEL_LANE0)).astype(F32)


def _overlap_t(seq, n_cmp):
    n = np.arange(n_cmp)
    j = np.arange(seq // SEL_BLOCK)
    cs, ce = n * CMP_STRIDE, n * CMP_STRIDE + CMP_BLOCK
    ss, se = j * SEL_BLOCK, j * SEL_BLOCK + SEL_BLOCK
    real = n < (seq // CMP_STRIDE - CMP_BLOCK // CMP_STRIDE + 1)
    ovl = (cs[None, :] < se[:, None]) & (ce[None, :] > ss[:, None]) & real[None, :]
    return jnp.asarray(ovl.astype(np.float32), dtype=BF16)


def _expand_cmp_w1(w1):
    half = CMP_BLOCK // CMP_STRIDE
    w = w1.reshape(half, CMP_STRIDE, HEAD_DIM, CMP_HIDDEN)
    eye = jnp.eye(N_GROUPS, dtype=w1.dtype)
    big = jnp.einsum('ptdh,ge->ptgdeh', w, eye)
    return big.reshape(half, N_CHUNKS_ROW, N_GROUPS * CMP_HIDDEN).astype(BF16)


def _expand_pe(pe):
    half = CMP_BLOCK // CMP_STRIDE
    p = pe.reshape(half, CMP_STRIDE, 1, HEAD_DIM)
    return jnp.broadcast_to(p, (half, CMP_STRIDE, N_GROUPS, HEAD_DIM)).reshape(half, N_CHUNKS_ROW)


def kernel(x, ffn1_norm_g, ffn1_w_gate, ffn1_w_up, ffn1_w_down, mix_norm_g, w_in, q_norm_g, k_norm_g, cmp_pe_k, cmp_pe_v, cmp_k_w1, cmp_k_w2, cmp_v_w1, cmp_v_w2, conv_w, w_attn_branch, w_conv_branch, w_out, ffn2_norm_g, ffn2_w_gate, ffn2_w_up, ffn2_w_down):
    batch, seq, d_model = x.shape
    depth = ffn1_norm_g.shape[0]
    n_chunks = seq // CMP_STRIDE
    assert d_model == D_MODEL and seq % TM == 0 and seq % (2 * TQ) == 0 and sum(FF_CHUNKS) == D_FF
    assert seq // SEL_BLOCK + SEL_LANE0 <= SLAB and n_chunks % SLAB == 0
    cos_t, sin_lo, sin_hi = _rope_tables(seq)
    onehot = _block_onehot(seq)
    ovl = _overlap_t(seq, n_chunks)
    xs = x.reshape(batch * seq, D_MODEL)

    for l in range(depth):
        xs = _ffn(xs, ffn1_norm_g[l][None], ffn1_w_gate[l].astype(BF16), ffn1_w_up[l].astype(BF16),
                  ffn1_w_down[l].astype(BF16))

        w = w_in[l]
        c_kv = ATTN_WIDTH
        c_gate = c_kv + N_KV_SLOTS * KV_WIDTH
        c_conv = c_gate + N_GATES
        c_merge = c_conv + 3 * CONV_WIDTH
        kv = [w[:, c_kv + s * KV_WIDTH:c_kv + (s + 1) * KV_WIDTH] for s in range(N_KV_SLOTS)]
        wgate = w[:, c_gate:c_conv].reshape(D_MODEL, 3, N_GROUPS, HEADS_PER_GROUP).transpose(0, 2, 1, 3)
        wgate = jnp.pad(wgate.reshape(D_MODEL, N_GROUPS, 3 * HEADS_PER_GROUP),
                        ((0, 0), (0, 0), (0, SLAB - 3 * HEADS_PER_GROUP))).reshape(D_MODEL, KV_PAD)
        w_b = jnp.concatenate([
            _pad_heads(w[:, :ATTN_WIDTH], N_HEADS),
            _pad_heads(kv[2], N_GROUPS), _pad_heads(kv[4], N_GROUPS),
            _pad_heads(kv[3], N_GROUPS), _pad_heads(kv[5], N_GROUPS),
            kv[0], kv[1], wgate, w[:, c_conv:c_merge]], axis=1).astype(BF16)
        q, ksel, kwin, vsel, vwin, kcmp, vcmp, gates, cu, bg = _inproj(
            xs, mix_norm_g[l][None], w_b, _pad_lanes(q_norm_g[l][None]), _pad_lanes(k_norm_g[l][1:3]),
            cos_t, sin_lo, sin_hi, onehot, seq)

        w2k = _pad_lanes(cmp_k_w2[l]).astype(BF16)
        w2v = _pad_lanes(cmp_v_w2[l]).astype(BF16)
        kc, vc = _compress(kcmp.reshape(batch, n_chunks, N_CHUNKS_ROW), vcmp.reshape(batch, n_chunks, N_CHUNKS_ROW),
                           _expand_pe(cmp_pe_k[l]), _expand_pe(cmp_pe_v[l]),
                           _expand_cmp_w1(cmp_k_w1[l]), _expand_cmp_w1(cmp_v_w1[l]), w2k, w2v,
                           _pad_lanes(k_norm_g[l][0:1]))

        a = _attention(q, kc, vc, ksel, vsel, kwin, vwin, gates, ovl, batch, seq)

        xs = _merge(xs, a, cu, bg, mix_norm_g[l][None], w[:, c_merge:].astype(BF16), w_attn_branch[l].astype(BF16),
                    w_conv_branch[l].astype(BF16), w_out[l].astype(BF16), conv_w[l], seq)

        xs = _ffn(xs, ffn2_norm_g[l][None], ffn2_w_gate[l].astype(BF16), ffn2_w_up[l].astype(BF16),
                  ffn2_w_down[l].astype(BF16))
    return xs.reshape(batch, seq, D_MODEL)
```

```python
import functools
import math

import numpy as np
import jax
import jax.numpy as jnp
from jax import lax
from jax.experimental import pallas as pl
from jax.experimental.pallas import tpu as pltpu

D_MODEL = 1024
N_HEADS = 8
N_GROUPS = 2
HEADS_PER_GROUP = N_HEADS // N_GROUPS
HEAD_DIM = 64
ATTN_WIDTH = N_HEADS * HEAD_DIM
KV_WIDTH = N_GROUPS * HEAD_DIM
N_KV_SLOTS = 6
ROPE_DIM = HEAD_DIM // 4
ROPE_HALF = ROPE_DIM // 2
ROPE_THETA = 500000.0
CMP_BLOCK = 32
CMP_STRIDE = 16
CMP_HIDDEN = 4 * HEAD_DIM
SEL_BLOCK = 64
N_SEL_BLOCKS = 16
WINDOW = 512
CONV_WIDTH = 512
CONV_KERNEL = 3
D_FF = 2816
NORM_EPS = 1e-6
MASK_VALUE = -1e30
FORCE_SCORE = 1e4
N_GATES = 3 * N_HEADS

LANES = 128
SLAB = LANES
MXU_DIM = 256
Q_PAD = N_HEADS * SLAB
KV_PAD = N_GROUPS * SLAB
V_ROWS = 80
VT_PAD = N_GROUPS * V_ROWS
SEL_BIAS = -float(2 ** 30)
SEL_LANE0 = HEAD_DIM
Q_SCALE = HEAD_DIM ** -0.5 * math.log2(math.e)

TM = 512
NORM_PARTS = 2
TQ = 256
FF_CHUNKS = (6 * MXU_DIM, 5 * MXU_DIM)
VMEM_LIMIT = 56 * 1024 * 1024

F32 = jnp.float32
BF16 = jnp.bfloat16


def _const_spec(shape):
    return pl.BlockSpec(shape, lambda *_: (0,) * len(shape), pipeline_mode=pl.Buffered(1))


def _params(n_axes):
    return pltpu.CompilerParams(dimension_semantics=("parallel",) * n_axes, vmem_limit_bytes=VMEM_LIMIT)


def _rms_rows(x, g):
    ms = jnp.mean(x * x, axis=-1, keepdims=True)
    return x * lax.rsqrt(ms + NORM_EPS) * g


def _norm_parts(x, g):
    part = x.shape[0] // NORM_PARTS
    parts = [_rms_rows(x[r:r + part], g).astype(BF16) for r in range(0, x.shape[0], part)]
    return parts, jnp.concatenate(parts, axis=0)


def _dot_parts(parts, w):
    return jnp.concatenate([jnp.dot(p, w, preferred_element_type=F32) for p in parts], axis=0)


def _ffn_kernel(x_ref, g_ref, wg_ref, wu_ref, wd_ref, o_ref):
    x = x_ref[...]
    h_parts, h = _norm_parts(x, g_ref[...])
    acc = jnp.zeros(x.shape, F32)
    c0 = 0
    for width in FF_CHUNKS:
        sl = slice(c0, c0 + width)
        if c0 == 0:
            gate = _dot_parts(h_parts, wg_ref[:, sl])
        else:
            gate = jnp.dot(h, wg_ref[:, sl], preferred_element_type=F32)
        up = jnp.dot(h, wu_ref[:, sl], preferred_element_type=F32)
        act = (gate * jax.nn.sigmoid(gate) * up).astype(BF16)
        acc = acc + jnp.dot(act, wd_ref[sl, :], preferred_element_type=F32)
        c0 += width
    o_ref[...] = x + 0.5 * acc


def _ffn(x, g, wg, wu, wd):
    n = x.shape[0]
    return pl.pallas_call(
        _ffn_kernel,
        grid=(n // TM,),
        in_specs=[
            pl.BlockSpec((TM, D_MODEL), lambda i: (i, 0)),
            _const_spec((1, D_MODEL)),
            _const_spec((D_MODEL, D_FF)),
            _const_spec((D_MODEL, D_FF)),
            _const_spec((D_FF, D_MODEL)),
        ],
        out_specs=pl.BlockSpec((TM, D_MODEL), lambda i: (i, 0)),
        out_shape=jax.ShapeDtypeStruct((n, D_MODEL), F32),
        compiler_params=_params(1),
        name="ffn",
    )(x, g, wg, wu, wd)


_C_Q = 0
_C_KSEL = _C_Q + Q_PAD
_C_KWIN = _C_KSEL + KV_PAD
_C_V = _C_KWIN + KV_PAD
_C_KCMP = _C_V + 2 * KV_WIDTH
_C_VCMP = _C_KCMP + KV_WIDTH
_C_GATE = _C_VCMP + KV_WIDTH
_C_CONV = _C_GATE + KV_PAD
_C_END = _C_CONV + 3 * CONV_WIDTH
N_CHUNKS_ROW = CMP_STRIDE * KV_WIDTH


def _norm_rope_slab(x, g, cos, sin_lo, sin_hi):
    ms = jnp.sum(x * x, axis=-1, keepdims=True) * (1.0 / HEAD_DIM)
    y = x * lax.rsqrt(ms + NORM_EPS) * g
    return y * cos + pltpu.roll(y, ROPE_HALF, 1) * sin_hi + pltpu.roll(y, SLAB - ROPE_HALF, 1) * sin_lo


def _inproj_kernel(x_ref, g_ref, w_ref, qg_ref, kg_ref, cos_ref, slo_ref, shi_ref, oh_ref,
                   q_ref, ksel_ref, kwin_ref, vsel_ref, vwin_ref, kcmp_ref, vcmp_ref, gate_ref, cu_ref, bg_ref,
                   stage_ref):
    h_parts, h = _norm_parts(x_ref[...], g_ref[...])
    cos, slo, shi = cos_ref[...], slo_ref[...], shi_ref[...]

    def proj(c0, width):
        return jnp.dot(h, w_ref[:, c0:c0 + width], preferred_element_type=F32)

    def emit_v():
        v_t = proj(_C_V, 2 * KV_WIDTH).T
        tail = jnp.where(lax.broadcasted_iota(jnp.int32, (V_ROWS - HEAD_DIM, TM), 0) == 0, 1.0, 0.0)
        for b, v_ref in enumerate((vsel_ref, vwin_ref)):
            for g in range(N_GROUPS):
                r0 = (b * N_GROUPS + g) * HEAD_DIM
                v_ref[g * V_ROWS:(g + 1) * V_ROWS, :] = jnp.concatenate([v_t[r0:r0 + HEAD_DIM], tail],
                                                                        axis=0).astype(BF16)

    def emit_q():
        q = _dot_parts(h_parts, w_ref[:, _C_Q:_C_Q + Q_PAD])
        qg = qg_ref[...]
        for hd in range(N_HEADS):
            sl = slice(hd * SLAB, (hd + 1) * SLAB)
            y = _norm_rope_slab(q[:, sl], qg, cos, slo, shi) * Q_SCALE
            q_ref[:, sl] = y.astype(BF16)

    def emit_k():
        ksel = proj(_C_KSEL, KV_PAD)
        kwin = proj(_C_KWIN, KV_PAD)
        oh = oh_ref[...]
        off = jnp.where(lax.broadcasted_iota(jnp.int32, (1, SLAB), 1) == HEAD_DIM, SEL_BIAS, 0.0)
        for g in range(N_GROUPS):
            sl = slice(g * SLAB, (g + 1) * SLAB)
            ksel_ref[:, sl] = (_norm_rope_slab(ksel[:, sl], kg_ref[0:1, :], cos, slo, shi) + oh).astype(BF16)
            kwin_ref[:, sl] = (_norm_rope_slab(kwin[:, sl], kg_ref[1:2, :], cos, slo, shi) + off).astype(BF16)

    def emit_cmp():
        kv_cmp = proj(_C_KCMP, 2 * KV_WIDTH)
        for c, out_ref in enumerate((kcmp_ref, vcmp_ref)):
            stage_ref[c] = kv_cmp[:, c * KV_WIDTH:(c + 1) * KV_WIDTH]
            for tok in range(CMP_STRIDE):
                out_ref[:, tok * KV_WIDTH:(tok + 1) * KV_WIDTH] = stage_ref[c, pl.ds(tok, TM // CMP_STRIDE,
                                                                                     stride=CMP_STRIDE), :]

    def emit_gates():
        gate_ref[...] = jax.nn.sigmoid(proj(_C_GATE, KV_PAD))

    def emit_conv():
        conv = proj(_C_CONV, 3 * CONV_WIDTH)
        bg_ref[...] = conv[:, :CONV_WIDTH]
        cu_ref[...] = conv[:, CONV_WIDTH:2 * CONV_WIDTH] * conv[:, 2 * CONV_WIDTH:]

    for emit in (emit_q, emit_k, emit_v, emit_cmp, emit_gates, emit_conv):
        emit()


def _inproj(x1, g, w_b, qg, kg, cos, slo, shi, oh, seq):
    n = x1.shape[0]
    tiles_per_seq = seq // TM
    row = lambda i: (i, 0)
    pos = lambda i: (i % tiles_per_seq, 0)
    vt = (pl.BlockSpec((None, VT_PAD, TM), lambda i: (i // tiles_per_seq, 0, i % tiles_per_seq)),
          jax.ShapeDtypeStruct((n // seq, VT_PAD, seq), BF16))
    chunks = (pl.BlockSpec((None, TM // CMP_STRIDE, N_CHUNKS_ROW),
                           lambda i: (i // tiles_per_seq, i % tiles_per_seq, 0)),
              jax.ShapeDtypeStruct((n // seq, seq // CMP_STRIDE, N_CHUNKS_ROW), F32))
    rows = lambda width, dtype: (pl.BlockSpec((TM, width), row), jax.ShapeDtypeStruct((n, width), dtype))
    outs = [rows(Q_PAD, BF16), rows(KV_PAD, BF16), rows(KV_PAD, BF16), vt, vt, chunks, chunks,
            rows(KV_PAD, F32), rows(CONV_WIDTH, F32), rows(CONV_WIDTH, F32)]
    return pl.pallas_call(
        _inproj_kernel,
        grid=(n // TM,),
        in_specs=[
            pl.BlockSpec((TM, D_MODEL), row),
            _const_spec((1, D_MODEL)),
            _const_spec((D_MODEL, _C_END)),
            _const_spec((1, SLAB)),
            _const_spec((2, SLAB)),
            pl.BlockSpec((TM, SLAB), pos),
            pl.BlockSpec((TM, SLAB), pos),
            pl.BlockSpec((TM, SLAB), pos),
            pl.BlockSpec((TM, SLAB), pos),
        ],
        out_specs=[spec for spec, _ in outs],
        out_shape=[shape for _, shape in outs],
        scratch_shapes=[pltpu.VMEM((2, TM, KV_WIDTH), F32)],
        compiler_params=_params(1),
        name="inproj",
    )(x1, g, w_b, qg, kg, cos, slo, shi, oh)


def _compress_kernel(kc_ref, vc_ref, pek_ref, pev_ref, wk1_ref, wv1_ref, wk2_ref, wv2_ref, kg_ref,
                     ko_ref, vo_ref):
    def branch(r_ref, pe_ref, w1_ref, w2_ref):
        r = r_ref[0]
        top = jnp.dot((r + pe_ref[0:1, :]).astype(BF16), w1_ref[0], preferred_element_type=F32)
        bot = jnp.dot((r + pe_ref[1:2, :]).astype(BF16), w1_ref[1], preferred_element_type=F32)
        hid = top + pltpu.roll(bot, bot.shape[0] - 1, 0)
        hid = (hid * jax.nn.sigmoid(hid)).astype(BF16)
        return [jnp.dot(hid[:, g * CMP_HIDDEN:(g + 1) * CMP_HIDDEN], w2_ref[...], preferred_element_type=F32)
                for g in range(N_GROUPS)]

    k = branch(kc_ref, pek_ref, wk1_ref, wk2_ref)
    v = branch(vc_ref, pev_ref, wv1_ref, wv2_ref)
    for g in range(N_GROUPS):
        ms = jnp.sum(k[g] * k[g], axis=-1, keepdims=True) * (1.0 / HEAD_DIM)
        ko_ref[0, g] = (k[g] * lax.rsqrt(ms + NORM_EPS) * kg_ref[...]).astype(BF16)
        vo_ref[0, g] = v[g].T.astype(BF16)


def _compress(kc, vc, pek, pev, wk1, wv1, wk2, wv2, kg):
    b, n_chunks, _ = kc.shape
    blk = pl.BlockSpec((1, n_chunks, N_CHUNKS_ROW), lambda i: (i, 0, 0))
    k_out = pl.BlockSpec((1, N_GROUPS, n_chunks, SLAB), lambda i: (i, 0, 0, 0))
    v_out = pl.BlockSpec((1, N_GROUPS, SLAB, n_chunks), lambda i: (i, 0, 0, 0))
    return pl.pallas_call(
        _compress_kernel,
        grid=(b,),
        in_specs=[blk, blk,
                  _const_spec((2, N_CHUNKS_ROW)), _const_spec((2, N_CHUNKS_ROW)),
                  _const_spec((2, N_CHUNKS_ROW, N_GROUPS * CMP_HIDDEN)),
                  _const_spec((2, N_CHUNKS_ROW, N_GROUPS * CMP_HIDDEN)),
                  _const_spec((CMP_HIDDEN, SLAB)), _const_spec((CMP_HIDDEN, SLAB)),
                  _const_spec((1, SLAB))],
        out_specs=[k_out, v_out],
        out_shape=[jax.ShapeDtypeStruct((b, N_GROUPS, n_chunks, SLAB), BF16),
                   jax.ShapeDtypeStruct((b, N_GROUPS, SLAB, n_chunks), BF16)],
        compiler_params=_params(1),
        name="compress",
    )(kc, vc, pek, pev, wk1, wv1, wk2, wv2, kg)


N_QL = HEADS_PER_GROUP * TQ
N_EARLY_WINDOW_TILES = 3

def _attn_kernel(q_ref, kc_ref, vc_ref, ksel_ref, vsel_ref, kwin_ref, vwin_ref, gate_ref, ovl_ref, o_ref,
                 qa_ref, qw_ref, m_ref, acc_ref, mw_ref, accw_ref, *, n_tiles, n_sel_blocks, k_eff, n_early):
    t = pl.program_id(2)
    tile = (t, n_tiles - 1 - t)
    n_cmp = kc_ref.shape[0]

    def per_head(x):
        return jnp.concatenate([x] * HEADS_PER_GROUP, axis=1)

    def rows_of(x):
        return pl.ds(pl.multiple_of(tile[x] * TQ, TQ), TQ)

    def keys(kt):
        return pl.ds(pl.multiple_of(kt * TQ, TQ), TQ)

    q_t = [[q_ref[rows_of(x), h * SLAB:(h + 1) * SLAB].astype(F32).T[:HEAD_DIM] for h in range(HEADS_PER_GROUP)]
           for x in range(2)]

    def q_operand(x, extra_rows):
        n_pad = SLAB - HEAD_DIM - sum(r.shape[0] for r in extra_rows)
        tail = extra_rows + [jnp.zeros((n_pad, TQ), F32)]
        return jnp.concatenate([jnp.concatenate([q_t[x][h]] + tail, axis=0) for h in range(HEADS_PER_GROUP)],
                               axis=1).astype(BF16)

    q_all = [q_operand(x, []) for x in range(2)]
    qw_ref[0] = q_all[0]
    qw_ref[1] = q_operand(0, [jnp.ones((1, TQ), F32)])

    for mr, ar in ((m_ref, acc_ref), (mw_ref, accw_ref)):
        mr[...] = jnp.full(mr.shape, MASK_VALUE, F32)
        ar[...] = jnp.zeros(ar.shape, F32)

    key_r = lax.broadcasted_iota(jnp.int32, (TQ, TQ), 0)
    qry_c = lax.broadcasted_iota(jnp.int32, (TQ, TQ), 1)
    diff0 = per_head(qry_c - key_r)

    def step(mr, ar, x, s, v_t):
        for h in range(HEADS_PER_GROUP):
            sl = slice(h * TQ, (h + 1) * TQ)
            m_old = mr[x, :, sl]
            m_new = jnp.maximum(m_old, jnp.max(s[:, sl], axis=0, keepdims=True))
            p = jnp.exp2(s[:, sl] - m_new).astype(BF16)
            ar[x, :, sl] = (jnp.exp2(m_old - m_new) * ar[x, :, sl]
                            + jnp.dot(v_t, p, preferred_element_type=F32))
            mr[x, :, sl] = m_new

    def result(ar, x):
        acc = ar[x]
        return acc * (1.0 / acc[HEAD_DIM:HEAD_DIM + 1, :])

    def run(work, ahead=2):
        pending = [w[0]() for w in work[:ahead]]
        for i, (_, update) in enumerate(work):
            if i + ahead < len(work):
                pending.append(work[i + ahead][0]())
            update(pending.pop(0))

    def win_scores(x, back):
        kt_raw = tile[x] - back
        if x == 0 and back > 0:
            q_op = qw_ref[jnp.where(kt_raw >= 0, 0, 1)]
        else:
            q_op = q_all[x]
        s = jnp.dot(kwin_ref[keys(jnp.maximum(kt_raw, 0)), :], q_op, preferred_element_type=F32)
        if back == 0:
            s = jnp.where(diff0 >= 0, s, MASK_VALUE)
        elif back * TQ == WINDOW:
            s = jnp.where(diff0 < 0, s, MASK_VALUE)
        return s

    win_work = []
    for x in range(2):
        for back in range(WINDOW // TQ, -1, -1):
            win_work.append((
                functools.partial(win_scores, x, back),
                functools.partial(lambda x, back, s: step(mw_ref, accw_ref, x, s,
                                                          vwin_ref[:, keys(jnp.maximum(tile[x] - back, 0))]),
                                  x, back)))

    run(win_work[:n_early])

    lane = lax.broadcasted_iota(jnp.int32, (n_cmp, TQ), 1)
    cmp_end = lax.broadcasted_iota(jnp.int32, (n_cmp, TQ), 0) * CMP_STRIDE + (CMP_BLOCK - 1)
    valid_c = jnp.concatenate([per_head(cmp_end <= tile[x] * TQ + lane) for x in range(2)], axis=1)
    s = jnp.dot(kc_ref[...], jnp.concatenate(q_all, axis=1), preferred_element_type=F32)
    s = jnp.where(valid_c, s, MASK_VALUE)
    e = jnp.exp2(s - jnp.max(s, axis=0, keepdims=True))
    p = jnp.where(valid_c, e * (1.0 / jnp.sum(e, axis=0, keepdims=True)), 0.0)
    o_cmp = jnp.dot(vc_ref[...], p.astype(BF16), preferred_element_type=F32)
    qa_ref[0] = q_all[0]
    p_sum = p[:, N_QL:N_QL + TQ]
    for h in range(1, HEADS_PER_GROUP):
        p_sum = p_sum + p[:, N_QL + h * TQ:N_QL + (h + 1) * TQ]
    p_hi = p_sum.astype(BF16)
    p_lo = (p_sum - p_hi.astype(F32)).astype(BF16)
    ovl = ovl_ref[...]
    imp = (jnp.dot(ovl, p_hi, preferred_element_type=F32)
           + jnp.dot(ovl, p_lo, preferred_element_type=F32))
    j_idx = lax.broadcasted_iota(jnp.int32, (n_sel_blocks, TQ), 0)
    pos_t = tile[1] * TQ + lax.broadcasted_iota(jnp.int32, (n_sel_blocks, TQ), 1)
    cur = pos_t // SEL_BLOCK
    imp = jnp.where(j_idx * SEL_BLOCK > pos_t, -1.0, imp)
    imp = jnp.where(j_idx == 0, FORCE_SCORE, imp)
    imp = jnp.where(j_idx == cur, FORCE_SCORE, imp)
    imp = jnp.where(j_idx == cur - 1, FORCE_SCORE, imp)
    grp = 8
    j_loc = lax.broadcasted_iota(jnp.int32, (grp, TQ), 0)
    bias = []
    for g0 in range(0, n_sel_blocks, grp):
        imp_g = imp[g0:g0 + grp]
        rank = jnp.zeros((grp, TQ), F32)
        for jp in range(n_sel_blocks):
            row = imp[jp:jp + 1, :]
            if jp >= g0 + grp - 1:
                rank = jnp.where(row > imp_g, rank + 1.0, rank)
            elif jp < g0:
                rank = jnp.where(row >= imp_g, rank + 1.0, rank)
            else:
                tie = jnp.where(j_loc > jp - g0, 1.0, 0.0)
                rank = rank + jnp.where(row > imp_g, 1.0, jnp.where(row == imp_g, tie, 0.0))
        bias.append(jnp.where(rank < k_eff, 0.0, SEL_BIAS))
    qa_ref[1] = q_operand(1, bias)

    sel_work = []
    for i in range(n_tiles - 1):
        x = jnp.where(i >= t, 1, 0)
        kt = jnp.where(i >= t, i - t, i)
        sel_work.append((
            functools.partial(lambda x, kt: jnp.dot(ksel_ref[keys(kt), :], qa_ref[x], preferred_element_type=F32),
                              x, kt),
            functools.partial(lambda x, kt, s: step(m_ref, acc_ref, x, s, vsel_ref[:, keys(kt)]), x, kt)))
    for x in range(2):
        sel_work.append((
            functools.partial(lambda x: jnp.where(
                diff0 >= 0, jnp.dot(ksel_ref[keys(tile[x]), :], qa_ref[x], preferred_element_type=F32), MASK_VALUE),
                x),
            functools.partial(lambda x, s: step(m_ref, acc_ref, x, s, vsel_ref[:, keys(tile[x])]), x)))

    rest = win_work[n_early:]
    work = []
    for i in range(max(len(sel_work), len(rest))):
        work += sel_work[i:i + 1] + rest[i:i + 1]
    run(work)

    for x in range(2):
        gates_t = gate_ref[rows_of(x), :].T
        o_sel = result(acc_ref, x)
        o_win = result(accw_ref, x)
        outs = []
        for h in range(HEADS_PER_GROUP):
            sl = slice(h * TQ, (h + 1) * TQ)
            o = (gates_t[h:h + 1] * o_cmp[:HEAD_DIM, x * N_QL + h * TQ:x * N_QL + (h + 1) * TQ]
                 + gates_t[HEADS_PER_GROUP + h:HEADS_PER_GROUP + h + 1] * o_sel[:HEAD_DIM, sl]
                 + gates_t[2 * HEADS_PER_GROUP + h:2 * HEADS_PER_GROUP + h + 1] * o_win[:HEAD_DIM, sl])
            outs.append(o)
        for hp in range(HEADS_PER_GROUP // 2):
            pair = jnp.concatenate(outs[2 * hp:2 * hp + 2], axis=0).T
            o_ref[rows_of(x), hp * SLAB:(hp + 1) * SLAB] = pair.astype(BF16)


def _attention(q, kc, vc, ksel, vsel, kwin, vwin, gates, ovl, batch, seq):
    n = q.shape[0]
    n_cmp = kc.shape[2]
    n_sel_blocks = seq // SEL_BLOCK
    n_tiles = seq // TQ
    width = HEADS_PER_GROUP * HEAD_DIM
    bg = lambda b, g, t: (b, g)
    return pl.pallas_call(
        functools.partial(_attn_kernel, n_tiles=n_tiles, n_sel_blocks=n_sel_blocks,
                          k_eff=min(N_SEL_BLOCKS, n_sel_blocks), n_early=N_EARLY_WINDOW_TILES),
        grid=(batch, N_GROUPS, n_tiles // 2),
        in_specs=[
            pl.BlockSpec((seq, HEADS_PER_GROUP * SLAB), bg),
            pl.BlockSpec((None, None, n_cmp, SLAB), lambda b, g, t: (b, g, 0, 0)),
            pl.BlockSpec((None, None, SLAB, n_cmp), lambda b, g, t: (b, g, 0, 0)),
            pl.BlockSpec((seq, SLAB), bg),
            pl.BlockSpec((None, V_ROWS, seq), lambda b, g, t: (b, g, 0)),
            pl.BlockSpec((seq, SLAB), bg),
            pl.BlockSpec((None, V_ROWS, seq), lambda b, g, t: (b, g, 0)),
            pl.BlockSpec((seq, SLAB), bg),
            _const_spec((n_sel_blocks, n_cmp)),
        ],
        out_specs=pl.BlockSpec((seq, width), bg),
        out_shape=jax.ShapeDtypeStruct((n, N_GROUPS * width), BF16),
        scratch_shapes=[pltpu.VMEM((2, SLAB, N_QL), BF16), pltpu.VMEM((2, SLAB, N_QL), BF16),
                        pltpu.VMEM((2, 1, N_QL), F32), pltpu.VMEM((2, V_ROWS, N_QL), F32),
                        pltpu.VMEM((2, 1, N_QL), F32), pltpu.VMEM((2, V_ROWS, N_QL), F32)],
        compiler_params=_params(3),
        name="attention",
    )(q, kc, vc, ksel, vsel, kwin, vwin, gates, ovl)


HALO = 8


def _merge_kernel(x_ref, a_ref, cu_ref, halo_ref, bg_ref, g_ref, wm_ref, wa_ref, wc_ref, wo_ref, cw_ref, o_ref,
                  *, tiles_per_seq):
    x = x_ref[...]
    h_parts, _ = _norm_parts(x, g_ref[...])
    mg = jax.nn.sigmoid(_dot_parts(h_parts, wm_ref[...]))
    att = jnp.dot(a_ref[...], wa_ref[...], preferred_element_type=F32)

    cu = cu_ref[...]
    first = pl.program_id(0) % tiles_per_seq == 0
    halo = jnp.where(first, 0.0, halo_ref[...])
    r = lax.broadcasted_iota(jnp.int32, cu.shape, 0)
    prev1 = jnp.where(r == 0, halo[HALO - 1:HALO, :], pltpu.roll(cu, 1, 0))
    prev2 = jnp.where(r == 0, halo[HALO - 2:HALO - 1, :],
                      jnp.where(r == 1, halo[HALO - 1:HALO, :], pltpu.roll(cu, 2, 0)))
    conv = cw_ref[0:1, :] * prev2 + cw_ref[1:2, :] * prev1 + cw_ref[2:3, :] * cu
    c = (bg_ref[...] * conv).astype(BF16)
    cb = jnp.dot(c, wc_ref[...], preferred_element_type=F32)

    merged = (mg[:, :D_MODEL] * att + mg[:, D_MODEL:] * cb).astype(BF16)
    o_ref[...] = x + jnp.dot(merged, wo_ref[...], preferred_element_type=F32)


def _merge(x1, a, cu, bg, g, wm, wa, wc, wo, cw, seq):
    n = x1.shape[0]
    tiles_per_seq = seq // TM
    row = lambda i: (i, 0)
    return pl.pallas_call(
        functools.partial(_merge_kernel, tiles_per_seq=tiles_per_seq),
        grid=(n // TM,),
        in_specs=[
            pl.BlockSpec((TM, D_MODEL), row),
            pl.BlockSpec((TM, ATTN_WIDTH), row),
            pl.BlockSpec((TM, CONV_WIDTH), row),
            pl.BlockSpec((HALO, CONV_WIDTH), lambda i: (jnp.maximum(i * (TM // HALO) - 1, 0), 0)),
            pl.BlockSpec((TM, CONV_WIDTH), row),
            _const_spec((1, D_MODEL)),
            _const_spec((D_MODEL, 2 * D_MODEL)),
            _const_spec((ATTN_WIDTH, D_MODEL)),
            _const_spec((CONV_WIDTH, D_MODEL)),
            _const_spec((D_MODEL, D_MODEL)),
            _const_spec((CONV_KERNEL, CONV_WIDTH)),
        ],
        out_specs=pl.BlockSpec((TM, D_MODEL), row),
        out_shape=jax.ShapeDtypeStruct((n, D_MODEL), F32),
        compiler_params=_params(1),
        name="merge",
    )(x1, a, cu, cu, bg, g, wm, wa, wc, wo, cw)


def _pad_heads(w, n_heads):
    rows = w.shape[0]
    w = w.reshape(rows, n_heads, HEAD_DIM)
    return jnp.pad(w, ((0, 0), (0, 0), (0, SLAB - HEAD_DIM))).reshape(rows, n_heads * SLAB)


def _pad_lanes(v):
    return jnp.pad(v, ((0, 0), (0, SLAB - v.shape[-1])))


def _rope_tables(seq):
    inv_freq = ROPE_THETA ** (-jnp.arange(0, ROPE_DIM, 2, dtype=F32) / ROPE_DIM)
    ang = jnp.arange(seq, dtype=F32)[:, None] * inv_freq[None, :]
    cos, sin = jnp.cos(ang), jnp.sin(ang)
    zeros = jnp.zeros((seq, SLAB - ROPE_DIM), F32)
    half0 = jnp.zeros((seq, ROPE_HALF), F32)
    cos_t = jnp.concatenate([cos, cos, jnp.ones((seq, SLAB - ROPE_DIM), F32)], axis=1)
    sin_lo = jnp.concatenate([-sin, half0, zeros], axis=1)
    sin_hi = jnp.concatenate([half0, sin, zeros], axis=1)
    return cos_t, sin_lo, sin_hi


def _block_onehot(seq):
    blk = jnp.arange(seq, dtype=jnp.int32) // SEL_BLOCK
    return (jnp.arange(SLAB, dtype=jnp.int32)[None, :] == (blk[:, None] + SEL_LANE0)).astype(F32)


def _overlap_t(seq, n_cmp):
    n = np.arange(n_cmp)
    j = np.arange(seq // SEL_BLOCK)
    cs, ce = n * CMP_STRIDE, n * CMP_STRIDE + CMP_BLOCK
    ss, se = j * SEL_BLOCK, j * SEL_BLOCK + SEL_BLOCK
    real = n < (seq // CMP_STRIDE - CMP_BLOCK // CMP_STRIDE + 1)
    ovl = (cs[None, :] < se[:, None]) & (ce[None, :] > ss[:, None]) & real[None, :]
    return jnp.asarray(ovl.astype(np.float32), dtype=BF16)


def _expand_cmp_w1(w1):
    half = CMP_BLOCK // CMP_STRIDE
    w = w1.reshape(half, CMP_STRIDE, HEAD_DIM, CMP_HIDDEN)
    eye = jnp.eye(N_GROUPS, dtype=w1.dtype)
    big = jnp.einsum('ptdh,ge->ptgdeh', w, eye)
    return big.reshape(half, N_CHUNKS_ROW, N_GROUPS * CMP_HIDDEN).astype(BF16)


def _expand_pe(pe):
    half = CMP_BLOCK // CMP_STRIDE
    p = pe.reshape(half, CMP_STRIDE, 1, HEAD_DIM)
    return jnp.broadcast_to(p, (half, CMP_STRIDE, N_GROUPS, HEAD_DIM)).reshape(half, N_CHUNKS_ROW)


def kernel(x, ffn1_norm_g, ffn1_w_gate, ffn1_w_up, ffn1_w_down, mix_norm_g, w_in, q_norm_g, k_norm_g, cmp_pe_k, cmp_pe_v, cmp_k_w1, cmp_k_w2, cmp_v_w1, cmp_v_w2, conv_w, w_attn_branch, w_conv_branch, w_out, ffn2_norm_g, ffn2_w_gate, ffn2_w_up, ffn2_w_down):
    batch, seq, d_model = x.shape
    depth = ffn1_norm_g.shape[0]
    n_chunks = seq // CMP_STRIDE
    assert d_model == D_MODEL and seq % TM == 0 and seq % (2 * TQ) == 0 and sum(FF_CHUNKS) == D_FF
    assert seq // SEL_BLOCK + SEL_LANE0 <= SLAB and n_chunks % SLAB == 0
    cos_t, sin_lo, sin_hi = _rope_tables(seq)
    onehot = _block_onehot(seq)
    ovl = _overlap_t(seq, n_chunks)
    xs = x.reshape(batch * seq, D_MODEL)

    for l in range(depth):
        xs = _ffn(xs, ffn1_norm_g[l][None], ffn1_w_gate[l].astype(BF16), ffn1_w_up[l].astype(BF16),
                  ffn1_w_down[l].astype(BF16))

        w = w_in[l]
        c_kv = ATTN_WIDTH
        c_gate = c_kv + N_KV_SLOTS * KV_WIDTH
        c_conv = c_gate + N_GATES
        c_merge = c_conv + 3 * CONV_WIDTH
        kv = [w[:, c_kv + s * KV_WIDTH:c_kv + (s + 1) * KV_WIDTH] for s in range(N_KV_SLOTS)]
        wgate = w[:, c_gate:c_conv].reshape(D_MODEL, 3, N_GROUPS, HEADS_PER_GROUP).transpose(0, 2, 1, 3)
        wgate = jnp.pad(wgate.reshape(D_MODEL, N_GROUPS, 3 * HEADS_PER_GROUP),
                        ((0, 0), (0, 0), (0, SLAB - 3 * HEADS_PER_GROUP))).reshape(D_MODEL, KV_PAD)
        w_b = jnp.concatenate([
            _pad_heads(w[:, :ATTN_WIDTH], N_HEADS),
            _pad_heads(kv[2], N_GROUPS), _pad_heads(kv[4], N_GROUPS), kv[3], kv[5],
            kv[0], kv[1], wgate, w[:, c_conv:c_merge]], axis=1).astype(BF16)
        q, ksel, kwin, vsel, vwin, kcmp, vcmp, gates, cu, bg = _inproj(
            xs, mix_norm_g[l][None], w_b, _pad_lanes(q_norm_g[l][None]), _pad_lanes(k_norm_g[l][1:3]),
            cos_t, sin_lo, sin_hi, onehot, seq)

        w2k = _pad_lanes(cmp_k_w2[l]).astype(BF16)
        w2v = _pad_lanes(cmp_v_w2[l]).astype(BF16)
        kc, vc = _compress(kcmp, vcmp, _expand_pe(cmp_pe_k[l]), _expand_pe(cmp_pe_v[l]),
                           _expand_cmp_w1(cmp_k_w1[l]), _expand_cmp_w1(cmp_v_w1[l]), w2k, w2v,
                           _pad_lanes(k_norm_g[l][0:1]))

        a = _attention(q, kc, vc, ksel, vsel, kwin, vwin, gates, ovl, batch, seq)

        xs = _merge(xs, a, cu, bg, mix_norm_g[l][None], w[:, c_merge:].astype(BF16), w_attn_branch[l].astype(BF16),
                    w_conv_branch[l].astype(BF16), w_out[l].astype(BF16), conv_w[l], seq)

        xs = _ffn(xs, ffn2_norm_g[l][None], ffn2_w_gate[l].astype(BF16), ffn2_w_up[l].astype(BF16),
                  ffn2_w_down[l].astype(BF16))
    return xs.reshape(batch, seq, D_MODEL)
```

```python
import functools
import math

import numpy as np
import jax
import jax.numpy as jnp
from jax import lax
from jax.experimental import pallas as pl
from jax.experimental.pallas import tpu as pltpu

D_MODEL = 1024
N_HEADS = 8
N_GROUPS = 2
HEADS_PER_GROUP = N_HEADS // N_GROUPS
HEAD_DIM = 64
ATTN_WIDTH = N_HEADS * HEAD_DIM
KV_WIDTH = N_GROUPS * HEAD_DIM
N_KV_SLOTS = 6
ROPE_DIM = HEAD_DIM // 4
ROPE_HALF = ROPE_DIM // 2
ROPE_THETA = 500000.0
CMP_BLOCK = 32
CMP_STRIDE = 16
CMP_HIDDEN = 4 * HEAD_DIM
SEL_BLOCK = 64
N_SEL_BLOCKS = 16
WINDOW = 512
CONV_WIDTH = 512
CONV_KERNEL = 3
D_FF = 2816
NORM_EPS = 1e-6
MASK_VALUE = -1e30
FORCE_SCORE = 1e4
N_GATES = 3 * N_HEADS

LANES = 128
SLAB = LANES
MXU_DIM = 256
Q_PAD = N_HEADS * SLAB
KV_PAD = N_GROUPS * SLAB
V_ROWS = 80
VT_PAD = N_GROUPS * V_ROWS
SEL_BIAS = -float(2 ** 30)
SEL_LANE0 = HEAD_DIM
Q_SCALE = HEAD_DIM ** -0.5 * math.log2(math.e)

TM = 512
NORM_PARTS = 2
TQ = 256
FF_CHUNKS = (6 * MXU_DIM, 5 * MXU_DIM)
VMEM_LIMIT = 56 * 1024 * 1024

F32 = jnp.float32
BF16 = jnp.bfloat16


def _const_spec(shape):
    return pl.BlockSpec(shape, lambda *_: (0,) * len(shape), pipeline_mode=pl.Buffered(1))


def _params(n_axes):
    return pltpu.CompilerParams(dimension_semantics=("parallel",) * n_axes, vmem_limit_bytes=VMEM_LIMIT)


def _rms_rows(x, g):
    ms = jnp.mean(x * x, axis=-1, keepdims=True)
    return x * lax.rsqrt(ms + NORM_EPS) * g


def _norm_parts(x, g):
    part = x.shape[0] // NORM_PARTS
    parts = [_rms_rows(x[r:r + part], g).astype(BF16) for r in range(0, x.shape[0], part)]
    return parts, jnp.concatenate(parts, axis=0)


def _dot_parts(parts, w):
    return jnp.concatenate([jnp.dot(p, w, preferred_element_type=F32) for p in parts], axis=0)


def _ffn_kernel(x_ref, g_ref, wg_ref, wu_ref, wd_ref, o_ref):
    x = x_ref[...]
    h_parts, h = _norm_parts(x, g_ref[...])
    acc = jnp.zeros(x.shape, F32)
    c0 = 0
    for width in FF_CHUNKS:
        sl = slice(c0, c0 + width)
        if c0 == 0:
            gate = _dot_parts(h_parts, wg_ref[:, sl])
        else:
            gate = jnp.dot(h, wg_ref[:, sl], preferred_element_type=F32)
        up = jnp.dot(h, wu_ref[:, sl], preferred_element_type=F32)
        act = (gate * jax.nn.sigmoid(gate) * up).astype(BF16)
        acc = acc + jnp.dot(act, wd_ref[sl, :], preferred_element_type=F32)
        c0 += width
    o_ref[...] = x + 0.5 * acc


def _ffn(x, g, wg, wu, wd):
    n = x.shape[0]
    return pl.pallas_call(
        _ffn_kernel,
        grid=(n // TM,),
        in_specs=[
            pl.BlockSpec((TM, D_MODEL), lambda i: (i, 0)),
            _const_spec((1, D_MODEL)),
            _const_spec((D_MODEL, D_FF)),
            _const_spec((D_MODEL, D_FF)),
            _const_spec((D_FF, D_MODEL)),
        ],
        out_specs=pl.BlockSpec((TM, D_MODEL), lambda i: (i, 0)),
        out_shape=jax.ShapeDtypeStruct((n, D_MODEL), F32),
        compiler_params=_params(1),
        name="ffn",
    )(x, g, wg, wu, wd)


_C_Q = 0
_C_KSEL = _C_Q + Q_PAD
_C_KWIN = _C_KSEL + KV_PAD
_C_V = _C_KWIN + KV_PAD
_C_KCMP = _C_V + 2 * KV_WIDTH
_C_VCMP = _C_KCMP + KV_WIDTH
_C_GATE = _C_VCMP + KV_WIDTH
_C_CONV = _C_GATE + KV_PAD
_C_END = _C_CONV + 3 * CONV_WIDTH
N_CHUNKS_ROW = CMP_STRIDE * KV_WIDTH


def _norm_rope_slab(x, g, cos, sin_lo, sin_hi):
    ms = jnp.sum(x * x, axis=-1, keepdims=True) * (1.0 / HEAD_DIM)
    y = x * lax.rsqrt(ms + NORM_EPS) * g
    return y * cos + pltpu.roll(y, ROPE_HALF, 1) * sin_hi + pltpu.roll(y, SLAB - ROPE_HALF, 1) * sin_lo


def _inproj_kernel(x_ref, g_ref, w_ref, qg_ref, kg_ref, cos_ref, slo_ref, shi_ref, oh_ref,
                   q_ref, ksel_ref, kwin_ref, vsel_ref, vwin_ref, kcmp_ref, vcmp_ref, gate_ref, cu_ref, bg_ref,
                   stage_ref):
    h_parts, h = _norm_parts(x_ref[...], g_ref[...])
    cos, slo, shi = cos_ref[...], slo_ref[...], shi_ref[...]

    def proj(c0, width):
        return jnp.dot(h, w_ref[:, c0:c0 + width], preferred_element_type=F32)

    def emit_v():
        v_t = proj(_C_V, 2 * KV_WIDTH).T
        tail = jnp.where(lax.broadcasted_iota(jnp.int32, (V_ROWS - HEAD_DIM, TM), 0) == 0, 1.0, 0.0)
        for b, v_ref in enumerate((vsel_ref, vwin_ref)):
            for g in range(N_GROUPS):
                r0 = (b * N_GROUPS + g) * HEAD_DIM
                v_ref[g * V_ROWS:(g + 1) * V_ROWS, :] = jnp.concatenate([v_t[r0:r0 + HEAD_DIM], tail],
                                                                        axis=0).astype(BF16)

    def emit_q():
        q = _dot_parts(h_parts, w_ref[:, _C_Q:_C_Q + Q_PAD])
        qg = qg_ref[...]
        for hd in range(N_HEADS):
            sl = slice(hd * SLAB, (hd + 1) * SLAB)
            y = _norm_rope_slab(q[:, sl], qg, cos, slo, shi) * Q_SCALE
            q_ref[:, sl] = y.astype(BF16)

    def emit_k():
        ksel = proj(_C_KSEL, KV_PAD)
        kwin = proj(_C_KWIN, KV_PAD)
        oh = oh_ref[...]
        off = jnp.where(lax.broadcasted_iota(jnp.int32, (1, SLAB), 1) == HEAD_DIM, SEL_BIAS, 0.0)
        for g in range(N_GROUPS):
            sl = slice(g * SLAB, (g + 1) * SLAB)
            ksel_ref[:, sl] = (_norm_rope_slab(ksel[:, sl], kg_ref[0:1, :], cos, slo, shi) + oh).astype(BF16)
            kwin_ref[:, sl] = (_norm_rope_slab(kwin[:, sl], kg_ref[1:2, :], cos, slo, shi) + off).astype(BF16)

    def emit_cmp():
        kv_cmp = proj(_C_KCMP, 2 * KV_WIDTH)
        for c, out_ref in enumerate((kcmp_ref, vcmp_ref)):
            stage_ref[c] = kv_cmp[:, c * KV_WIDTH:(c + 1) * KV_WIDTH]
            for tok in range(CMP_STRIDE):
                out_ref[:, tok * KV_WIDTH:(tok + 1) * KV_WIDTH] = stage_ref[c, pl.ds(tok, TM // CMP_STRIDE,
                                                                                     stride=CMP_STRIDE), :]

    def emit_gates():
        gate_ref[...] = jax.nn.sigmoid(proj(_C_GATE, KV_PAD))

    def emit_conv():
        conv = proj(_C_CONV, 3 * CONV_WIDTH)
        bg_ref[...] = conv[:, :CONV_WIDTH]
        cu_ref[...] = conv[:, CONV_WIDTH:2 * CONV_WIDTH] * conv[:, 2 * CONV_WIDTH:]

    for emit in (emit_q, emit_k, emit_v, emit_cmp, emit_gates, emit_conv):
        emit()


def _inproj(x1, g, w_b, qg, kg, cos, slo, shi, oh, seq):
    n = x1.shape[0]
    tiles_per_seq = seq // TM
    row = lambda i: (i, 0)
    pos = lambda i: (i % tiles_per_seq, 0)
    vt = (pl.BlockSpec((None, VT_PAD, TM), lambda i: (i // tiles_per_seq, 0, i % tiles_per_seq)),
          jax.ShapeDtypeStruct((n // seq, VT_PAD, seq), BF16))
    chunks = (pl.BlockSpec((None, TM // CMP_STRIDE, N_CHUNKS_ROW),
                           lambda i: (i // tiles_per_seq, i % tiles_per_seq, 0)),
              jax.ShapeDtypeStruct((n // seq, seq // CMP_STRIDE, N_CHUNKS_ROW), F32))
    rows = lambda width, dtype: (pl.BlockSpec((TM, width), row), jax.ShapeDtypeStruct((n, width), dtype))
    outs = [rows(Q_PAD, BF16), rows(KV_PAD, BF16), rows(KV_PAD, BF16), vt, vt, chunks, chunks,
            rows(KV_PAD, F32), rows(CONV_WIDTH, F32), rows(CONV_WIDTH, F32)]
    return pl.pallas_call(
        _inproj_kernel,
        grid=(n // TM,),
        in_specs=[
            pl.BlockSpec((TM, D_MODEL), row),
            _const_spec((1, D_MODEL)),
            _const_spec((D_MODEL, _C_END)),
            _const_spec((1, SLAB)),
            _const_spec((2, SLAB)),
            pl.BlockSpec((TM, SLAB), pos),
            pl.BlockSpec((TM, SLAB), pos),
            pl.BlockSpec((TM, SLAB), pos),
            pl.BlockSpec((TM, SLAB), pos),
        ],
        out_specs=[spec for spec, _ in outs],
        out_shape=[shape for _, shape in outs],
        scratch_shapes=[pltpu.VMEM((2, TM, KV_WIDTH), F32)],
        compiler_params=_params(1),
        name="inproj",
    )(x1, g, w_b, qg, kg, cos, slo, shi, oh)


def _compress_kernel(kc_ref, vc_ref, pek_ref, pev_ref, wk1_ref, wv1_ref, wk2_ref, wv2_ref, kg_ref,
                     ko_ref, vo_ref):
    def branch(r_ref, pe_ref, w1_ref, w2_ref):
        r = r_ref[0]
        top = jnp.dot((r + pe_ref[0:1, :]).astype(BF16), w1_ref[0], preferred_element_type=F32)
        bot = jnp.dot((r + pe_ref[1:2, :]).astype(BF16), w1_ref[1], preferred_element_type=F32)
        hid = top + pltpu.roll(bot, bot.shape[0] - 1, 0)
        hid = (hid * jax.nn.sigmoid(hid)).astype(BF16)
        return [jnp.dot(hid[:, g * CMP_HIDDEN:(g + 1) * CMP_HIDDEN], w2_ref[...], preferred_element_type=F32)
                for g in range(N_GROUPS)]

    k = branch(kc_ref, pek_ref, wk1_ref, wk2_ref)
    v = branch(vc_ref, pev_ref, wv1_ref, wv2_ref)
    for g in range(N_GROUPS):
        ms = jnp.sum(k[g] * k[g], axis=-1, keepdims=True) * (1.0 / HEAD_DIM)
        ko_ref[0, g] = (k[g] * lax.rsqrt(ms + NORM_EPS) * kg_ref[...]).astype(BF16)
        vo_ref[0, g] = v[g].T.astype(BF16)


def _compress(kc, vc, pek, pev, wk1, wv1, wk2, wv2, kg):
    b, n_chunks, _ = kc.shape
    blk = pl.BlockSpec((1, n_chunks, N_CHUNKS_ROW), lambda i: (i, 0, 0))
    k_out = pl.BlockSpec((1, N_GROUPS, n_chunks, SLAB), lambda i: (i, 0, 0, 0))
    v_out = pl.BlockSpec((1, N_GROUPS, SLAB, n_chunks), lambda i: (i, 0, 0, 0))
    return pl.pallas_call(
        _compress_kernel,
        grid=(b,),
        in_specs=[blk, blk,
                  _const_spec((2, N_CHUNKS_ROW)), _const_spec((2, N_CHUNKS_ROW)),
                  _const_spec((2, N_CHUNKS_ROW, N_GROUPS * CMP_HIDDEN)),
                  _const_spec((2, N_CHUNKS_ROW, N_GROUPS * CMP_HIDDEN)),
                  _const_spec((CMP_HIDDEN, SLAB)), _const_spec((CMP_HIDDEN, SLAB)),
                  _const_spec((1, SLAB))],
        out_specs=[k_out, v_out],
        out_shape=[jax.ShapeDtypeStruct((b, N_GROUPS, n_chunks, SLAB), BF16),
                   jax.ShapeDtypeStruct((b, N_GROUPS, SLAB, n_chunks), BF16)],
        compiler_params=_params(1),
        name="compress",
    )(kc, vc, pek, pev, wk1, wv1, wk2, wv2, kg)


N_QL = HEADS_PER_GROUP * TQ
N_EARLY_WINDOW_TILES = 3
SCORES_AHEAD = 8

def _attn_pair(t, q_ref, kc_ref, vc_ref, ksel_ref, vsel_ref, kwin_ref, vwin_ref, gate_ref, ovl_ref, o_ref,
               qa_ref, qw_ref, m_ref, acc_ref, mw_ref, accw_ref, *, n_tiles, n_sel_blocks, k_eff, n_early):
    tile = (t, n_tiles - 1 - t)
    n_cmp = kc_ref.shape[0]

    def per_head(x):
        return jnp.concatenate([x] * HEADS_PER_GROUP, axis=1)

    def rows_of(x):
        return pl.ds(pl.multiple_of(tile[x] * TQ, TQ), TQ)

    def keys(kt):
        return pl.ds(pl.multiple_of(kt * TQ, TQ), TQ)

    q_t = [[q_ref[rows_of(x), h * SLAB:(h + 1) * SLAB].astype(F32).T[:HEAD_DIM] for h in range(HEADS_PER_GROUP)]
           for x in range(2)]

    def q_operand(x, extra_rows):
        n_pad = SLAB - HEAD_DIM - sum(r.shape[0] for r in extra_rows)
        tail = extra_rows + [jnp.zeros((n_pad, TQ), F32)]
        return jnp.concatenate([jnp.concatenate([q_t[x][h]] + tail, axis=0) for h in range(HEADS_PER_GROUP)],
                               axis=1).astype(BF16)

    q_all = [q_operand(x, []) for x in range(2)]
    qw_ref[0] = q_all[0]
    qw_ref[1] = q_operand(0, [jnp.ones((1, TQ), F32)])

    for mr, ar in ((m_ref, acc_ref), (mw_ref, accw_ref)):
        mr[...] = jnp.full(mr.shape, MASK_VALUE, F32)
        ar[...] = jnp.zeros(ar.shape, F32)

    key_r = lax.broadcasted_iota(jnp.int32, (TQ, TQ), 0)
    qry_c = lax.broadcasted_iota(jnp.int32, (TQ, TQ), 1)

    def head_step(mr, ar, x, sl, s, v_t):
        m_old = mr[x, :, sl]
        m_new = jnp.maximum(m_old, jnp.max(s, axis=0, keepdims=True))
        p = jnp.exp2(s - m_new).astype(BF16)
        ar[x, :, sl] = jnp.exp2(m_old - m_new) * ar[x, :, sl] + jnp.dot(v_t, p, preferred_element_type=F32)
        mr[x, :, sl] = m_new

    def tile_work(k_ref, v_ref, kt, q_head, mask, mr, ar, x):
        work = []
        for h in range(HEADS_PER_GROUP):
            sl = slice(h * TQ, (h + 1) * TQ)
            work.append((
                functools.partial(lambda sl: mask(jnp.dot(k_ref[keys(kt), :], q_head(sl),
                                                          preferred_element_type=F32)), sl),
                functools.partial(lambda sl, s: head_step(mr, ar, x, sl, s, v_ref[:, keys(kt)]), sl)))
        return work

    def result(ar, x):
        acc = ar[x]
        return acc * (1.0 / acc[HEAD_DIM:HEAD_DIM + 1, :])

    def run(work, ahead=SCORES_AHEAD):
        pending = [w[0]() for w in work[:ahead]]
        for i, (_, update) in enumerate(work):
            if i + ahead < len(work):
                pending.append(work[i + ahead][0]())
            update(pending.pop(0))

    diff0 = qry_c - key_r
    no_mask = lambda s: s
    causal = lambda s: jnp.where(diff0 >= 0, s, MASK_VALUE)
    band = lambda s: jnp.where(diff0 < 0, s, MASK_VALUE)

    win_work = []
    for x in range(2):
        for back in range(WINDOW // TQ, -1, -1):
            kt_raw = tile[x] - back
            if x == 0 and back > 0:
                off = jnp.where(kt_raw >= 0, 0, 1)
                q_head = functools.partial(lambda off, sl: qw_ref[off, :, sl], off)
            else:
                q_head = functools.partial(lambda x, sl: q_all[x][:, sl], x)
            mask = causal if back == 0 else band if back * TQ == WINDOW else no_mask
            win_work.append(tile_work(kwin_ref, vwin_ref, jnp.maximum(kt_raw, 0), q_head, mask, mw_ref, accw_ref, x))

    flat = lambda tiles: [pair for tile_pairs in tiles for pair in tile_pairs]
    run(flat(win_work[:n_early]))

    lane = lax.broadcasted_iota(jnp.int32, (n_cmp, TQ), 1)
    cmp_end = lax.broadcasted_iota(jnp.int32, (n_cmp, TQ), 0) * CMP_STRIDE + (CMP_BLOCK - 1)
    valid_c = jnp.concatenate([per_head(cmp_end <= tile[x] * TQ + lane) for x in range(2)], axis=1)
    s = jnp.dot(kc_ref[...], jnp.concatenate(q_all, axis=1), preferred_element_type=F32)
    s = jnp.where(valid_c, s, MASK_VALUE)
    e = jnp.exp2(s - jnp.max(s, axis=0, keepdims=True))
    p = jnp.where(valid_c, e * (1.0 / jnp.sum(e, axis=0, keepdims=True)), 0.0)
    o_cmp = jnp.dot(vc_ref[...], p.astype(BF16), preferred_element_type=F32)
    qa_ref[0] = q_all[0]
    p_sum = p[:, N_QL:N_QL + TQ]
    for h in range(1, HEADS_PER_GROUP):
        p_sum = p_sum + p[:, N_QL + h * TQ:N_QL + (h + 1) * TQ]
    p_hi = p_sum.astype(BF16)
    p_lo = (p_sum - p_hi.astype(F32)).astype(BF16)
    ovl = ovl_ref[...]
    imp = (jnp.dot(ovl, p_hi, preferred_element_type=F32)
           + jnp.dot(ovl, p_lo, preferred_element_type=F32))
    j_idx = lax.broadcasted_iota(jnp.int32, (n_sel_blocks, TQ), 0)
    pos_t = tile[1] * TQ + lax.broadcasted_iota(jnp.int32, (n_sel_blocks, TQ), 1)
    cur = pos_t // SEL_BLOCK
    imp = jnp.where(j_idx * SEL_BLOCK > pos_t, -1.0, imp)
    imp = jnp.where(j_idx == 0, FORCE_SCORE, imp)
    imp = jnp.where(j_idx == cur, FORCE_SCORE, imp)
    imp = jnp.where(j_idx == cur - 1, FORCE_SCORE, imp)
    grp = 8
    j_loc = lax.broadcasted_iota(jnp.int32, (grp, TQ), 0)
    bias = []
    for g0 in range(0, n_sel_blocks, grp):
        imp_g = imp[g0:g0 + grp]
        rank = jnp.zeros((grp, TQ), F32)
        for jp in range(n_sel_blocks):
            row = imp[jp:jp + 1, :]
            if jp >= g0 + grp - 1:
                rank = jnp.where(row > imp_g, rank + 1.0, rank)
            elif jp < g0:
                rank = jnp.where(row >= imp_g, rank + 1.0, rank)
            else:
                tie = jnp.where(j_loc > jp - g0, 1.0, 0.0)
                rank = rank + jnp.where(row > imp_g, 1.0, jnp.where(row == imp_g, tie, 0.0))
        bias.append(jnp.where(rank < k_eff, 0.0, SEL_BIAS))
    qa_ref[1] = q_operand(1, bias)

    sel_work = []
    for i in range(n_tiles - 1):
        x = jnp.where(i >= t, 1, 0)
        kt = jnp.where(i >= t, i - t, i)
        sel_work.append(tile_work(ksel_ref, vsel_ref, kt, functools.partial(lambda x, sl: qa_ref[x, :, sl], x),
                                  no_mask, m_ref, acc_ref, x))
    for x in range(2):
        sel_work.append(tile_work(ksel_ref, vsel_ref, tile[x], functools.partial(lambda x, sl: qa_ref[x, :, sl], x),
                                  causal, m_ref, acc_ref, x))

    rest = win_work[n_early:]
    work = []
    for i in range(max(len(sel_work), len(rest))):
        if i < len(rest):
            work += [pair for both in zip(sel_work[i], rest[i]) for pair in both]
        else:
            work += sel_work[i]
    run(work)

    for x in range(2):
        gates_t = gate_ref[rows_of(x), :].T
        o_sel = result(acc_ref, x)
        o_win = result(accw_ref, x)
        outs = []
        for h in range(HEADS_PER_GROUP):
            sl = slice(h * TQ, (h + 1) * TQ)
            o = (gates_t[h:h + 1] * o_cmp[:HEAD_DIM, x * N_QL + h * TQ:x * N_QL + (h + 1) * TQ]
                 + gates_t[HEADS_PER_GROUP + h:HEADS_PER_GROUP + h + 1] * o_sel[:HEAD_DIM, sl]
                 + gates_t[2 * HEADS_PER_GROUP + h:2 * HEADS_PER_GROUP + h + 1] * o_win[:HEAD_DIM, sl])
            outs.append(o)
        for hp in range(HEADS_PER_GROUP // 2):
            pair = jnp.concatenate(outs[2 * hp:2 * hp + 2], axis=0).T
            o_ref[rows_of(x), hp * SLAB:(hp + 1) * SLAB] = pair.astype(BF16)


def _attn_kernel(*refs, n_tiles, **static):
    @pl.loop(0, n_tiles // 2)
    def _(t):
        _attn_pair(t, *refs, n_tiles=n_tiles, **static)


def _attention(q, kc, vc, ksel, vsel, kwin, vwin, gates, ovl, batch, seq):
    n = q.shape[0]
    n_cmp = kc.shape[2]
    n_sel_blocks = seq // SEL_BLOCK
    n_tiles = seq // TQ
    width = HEADS_PER_GROUP * HEAD_DIM
    bg = lambda b, g: (b, g)
    return pl.pallas_call(
        functools.partial(_attn_kernel, n_tiles=n_tiles, n_sel_blocks=n_sel_blocks,
                          k_eff=min(N_SEL_BLOCKS, n_sel_blocks), n_early=N_EARLY_WINDOW_TILES),
        grid=(batch, N_GROUPS),
        in_specs=[
            pl.BlockSpec((seq, HEADS_PER_GROUP * SLAB), bg),
            pl.BlockSpec((None, None, n_cmp, SLAB), lambda b, g: (b, g, 0, 0)),
            pl.BlockSpec((None, None, SLAB, n_cmp), lambda b, g: (b, g, 0, 0)),
            pl.BlockSpec((seq, SLAB), bg),
            pl.BlockSpec((None, V_ROWS, seq), lambda b, g: (b, g, 0)),
            pl.BlockSpec((seq, SLAB), bg),
            pl.BlockSpec((None, V_ROWS, seq), lambda b, g: (b, g, 0)),
            pl.BlockSpec((seq, SLAB), bg),
            _const_spec((n_sel_blocks, n_cmp)),
        ],
        out_specs=pl.BlockSpec((seq, width), bg),
        out_shape=jax.ShapeDtypeStruct((n, N_GROUPS * width), BF16),
        scratch_shapes=[pltpu.VMEM((2, SLAB, N_QL), BF16), pltpu.VMEM((2, SLAB, N_QL), BF16),
                        pltpu.VMEM((2, 1, N_QL), F32), pltpu.VMEM((2, V_ROWS, N_QL), F32),
                        pltpu.VMEM((2, 1, N_QL), F32), pltpu.VMEM((2, V_ROWS, N_QL), F32)],
        compiler_params=_params(2),
        name="attention",
    )(q, kc, vc, ksel, vsel, kwin, vwin, gates, ovl)


HALO = 8


def _merge_kernel(x_ref, a_ref, cu_ref, halo_ref, bg_ref, g_ref, wm_ref, wa_ref, wc_ref, wo_ref, cw_ref, o_ref,
                  *, tiles_per_seq):
    x = x_ref[...]
    h_parts, _ = _norm_parts(x, g_ref[...])
    mg = jax.nn.sigmoid(_dot_parts(h_parts, wm_ref[...]))
    att = jnp.dot(a_ref[...], wa_ref[...], preferred_element_type=F32)

    cu = cu_ref[...]
    first = pl.program_id(0) % tiles_per_seq == 0
    halo = jnp.where(first, 0.0, halo_ref[...])
    r = lax.broadcasted_iota(jnp.int32, cu.shape, 0)
    prev1 = jnp.where(r == 0, halo[HALO - 1:HALO, :], pltpu.roll(cu, 1, 0))
    prev2 = jnp.where(r == 0, halo[HALO - 2:HALO - 1, :],
                      jnp.where(r == 1, halo[HALO - 1:HALO, :], pltpu.roll(cu, 2, 0)))
    conv = cw_ref[0:1, :] * prev2 + cw_ref[1:2, :] * prev1 + cw_ref[2:3, :] * cu
    c = (bg_ref[...] * conv).astype(BF16)
    cb = jnp.dot(c, wc_ref[...], preferred_element_type=F32)

    merged = (mg[:, :D_MODEL] * att + mg[:, D_MODEL:] * cb).astype(BF16)
    o_ref[...] = x + jnp.dot(merged, wo_ref[...], preferred_element_type=F32)


def _merge(x1, a, cu, bg, g, wm, wa, wc, wo, cw, seq):
    n = x1.shape[0]
    tiles_per_seq = seq // TM
    row = lambda i: (i, 0)
    return pl.pallas_call(
        functools.partial(_merge_kernel, tiles_per_seq=tiles_per_seq),
        grid=(n // TM,),
        in_specs=[
            pl.BlockSpec((TM, D_MODEL), row),
            pl.BlockSpec((TM, ATTN_WIDTH), row),
            pl.BlockSpec((TM, CONV_WIDTH), row),
            pl.BlockSpec((HALO, CONV_WIDTH), lambda i: (jnp.maximum(i * (TM // HALO) - 1, 0), 0)),
            pl.BlockSpec((TM, CONV_WIDTH), row),
            _const_spec((1, D_MODEL)),
            _const_spec((D_MODEL, 2 * D_MODEL)),
            _const_spec((ATTN_WIDTH, D_MODEL)),
            _const_spec((CONV_WIDTH, D_MODEL)),
            _const_spec((D_MODEL, D_MODEL)),
            _const_spec((CONV_KERNEL, CONV_WIDTH)),
        ],
        out_specs=pl.BlockSpec((TM, D_MODEL), row),
        out_shape=jax.ShapeDtypeStruct((n, D_MODEL), F32),
        compiler_params=_params(1),
        name="merge",
    )(x1, a, cu, cu, bg, g, wm, wa, wc, wo, cw)


def _pad_heads(w, n_heads):
    rows = w.shape[0]
    w = w.reshape(rows, n_heads, HEAD_DIM)
    return jnp.pad(w, ((0, 0), (0, 0), (0, SLAB - HEAD_DIM))).reshape(rows, n_heads * SLAB)


def _pad_lanes(v):
    return jnp.pad(v, ((0, 0), (0, SLAB - v.shape[-1])))


def _rope_tables(seq):
    inv_freq = ROPE_THETA ** (-jnp.arange(0, ROPE_DIM, 2, dtype=F32) / ROPE_DIM)
    ang = jnp.arange(seq, dtype=F32)[:, None] * inv_freq[None, :]
    cos, sin = jnp.cos(ang), jnp.sin(ang)
    zeros = jnp.zeros((seq, SLAB - ROPE_DIM), F32)
    half0 = jnp.zeros((seq, ROPE_HALF), F32)
    cos_t = jnp.concatenate([cos, cos, jnp.ones((seq, SLAB - ROPE_DIM), F32)], axis=1)
    sin_lo = jnp.concatenate([-sin, half0, zeros], axis=1)
    sin_hi = jnp.concatenate([half0, sin, zeros], axis=1)
    return cos_t, sin_lo, sin_hi


def _block_onehot(seq):
    blk = jnp.arange(seq, dtype=jnp.int32) // SEL_BLOCK
    return (jnp.arange(SLAB, dtype=jnp.int32)[None, :] == (blk[:, None] + SEL_LANE0)).astype(F32)


def _overlap_t(seq, n_cmp):
    n = np.arange(n_cmp)
    j = np.arange(seq // SEL_BLOCK)
    cs, ce = n * CMP_STRIDE, n * CMP_STRIDE + CMP_BLOCK
    ss, se = j * SEL_BLOCK, j * SEL_BLOCK + SEL_BLOCK
    real = n < (seq // CMP_STRIDE - CMP_BLOCK // CMP_STRIDE + 1)
    ovl = (cs[None, :] < se[:, None]) & (ce[None, :] > ss[:, None]) & real[None, :]
    return jnp.asarray(ovl.astype(np.float32), dtype=BF16)


def _expand_cmp_w1(w1):
    half = CMP_BLOCK // CMP_STRIDE
    w = w1.astype(BF16).reshape(half, CMP_STRIDE, 1, HEAD_DIM, 1, CMP_HIDDEN)
    same_group = jnp.eye(N_GROUPS, dtype=bool).reshape(1, 1, N_GROUPS, 1, N_GROUPS, 1)
    big = jnp.where(same_group, w, jnp.zeros((), BF16))
    return big.reshape(half, N_CHUNKS_ROW, N_GROUPS * CMP_HIDDEN)


def _expand_pe(pe):
    half = CMP_BLOCK // CMP_STRIDE
    p = pe.reshape(half, CMP_STRIDE, 1, HEAD_DIM)
    return jnp.broadcast_to(p, (half, CMP_STRIDE, N_GROUPS, HEAD_DIM)).reshape(half, N_CHUNKS_ROW)


def kernel(x, ffn1_norm_g, ffn1_w_gate, ffn1_w_up, ffn1_w_down, mix_norm_g, w_in, q_norm_g, k_norm_g, cmp_pe_k, cmp_pe_v, cmp_k_w1, cmp_k_w2, cmp_v_w1, cmp_v_w2, conv_w, w_attn_branch, w_conv_branch, w_out, ffn2_norm_g, ffn2_w_gate, ffn2_w_up, ffn2_w_down):
    batch, seq, d_model = x.shape
    depth = ffn1_norm_g.shape[0]
    n_chunks = seq // CMP_STRIDE
    assert d_model == D_MODEL and seq % TM == 0 and seq % (2 * TQ) == 0 and sum(FF_CHUNKS) == D_FF
    assert seq // SEL_BLOCK + SEL_LANE0 <= SLAB and n_chunks % SLAB == 0 and WINDOW % TQ == 0
    assert seq // 2 <= min(N_SEL_BLOCKS, seq // SEL_BLOCK) * SEL_BLOCK
    cos_t, sin_lo, sin_hi = _rope_tables(seq)
    onehot = _block_onehot(seq)
    ovl = _overlap_t(seq, n_chunks)
    xs = x.reshape(batch * seq, D_MODEL)

    for l in range(depth):
        xs = _ffn(xs, ffn1_norm_g[l][None], ffn1_w_gate[l].astype(BF16), ffn1_w_up[l].astype(BF16),
                  ffn1_w_down[l].astype(BF16))

        w = w_in[l].astype(BF16)
        c_kv = ATTN_WIDTH
        c_gate = c_kv + N_KV_SLOTS * KV_WIDTH
        c_conv = c_gate + N_GATES
        c_merge = c_conv + 3 * CONV_WIDTH
        kv = [w[:, c_kv + s * KV_WIDTH:c_kv + (s + 1) * KV_WIDTH] for s in range(N_KV_SLOTS)]
        wgate = w[:, c_gate:c_conv].reshape(D_MODEL, 3, N_GROUPS, HEADS_PER_GROUP).transpose(0, 2, 1, 3)
        wgate = jnp.pad(wgate.reshape(D_MODEL, N_GROUPS, 3 * HEADS_PER_GROUP),
                        ((0, 0), (0, 0), (0, SLAB - 3 * HEADS_PER_GROUP))).reshape(D_MODEL, KV_PAD)
        w_b = jnp.concatenate([
            _pad_heads(w[:, :ATTN_WIDTH], N_HEADS),
            _pad_heads(kv[2], N_GROUPS), _pad_heads(kv[4], N_GROUPS), kv[3], kv[5],
            kv[0], kv[1], wgate, w[:, c_conv:c_merge]], axis=1)
        q, ksel, kwin, vsel, vwin, kcmp, vcmp, gates, cu, bg = _inproj(
            xs, mix_norm_g[l][None], w_b, _pad_lanes(q_norm_g[l][None]), _pad_lanes(k_norm_g[l][1:3]),
            cos_t, sin_lo, sin_hi, onehot, seq)

        w2k = _pad_lanes(cmp_k_w2[l]).astype(BF16)
        w2v = _pad_lanes(cmp_v_w2[l]).astype(BF16)
        kc, vc = _compress(kcmp, vcmp, _expand_pe(cmp_pe_k[l]), _expand_pe(cmp_pe_v[l]),
                           _expand_cmp_w1(cmp_k_w1[l]), _expand_cmp_w1(cmp_v_w1[l]), w2k, w2v,
                           _pad_lanes(k_norm_g[l][0:1]))

        a = _attention(q, kc, vc, ksel, vsel, kwin, vwin, gates, ovl, batch, seq)

        xs = _merge(xs, a, cu, bg, mix_norm_g[l][None], w[:, c_merge:], w_attn_branch[l].astype(BF16),
                    w_conv_branch[l].astype(BF16), w_out[l].astype(BF16), conv_w[l], seq)

        xs = _ffn(xs, ffn2_norm_g[l][None], ffn2_w_gate[l].astype(BF16), ffn2_w_up[l].astype(BF16),
                  ffn2_w_down[l].astype(BF16))
    return xs.reshape(batch, seq, D_MODEL)
```

```python
import functools
import math

import numpy as np
import jax
import jax.numpy as jnp
from jax import lax
from jax.experimental import pallas as pl
from jax.experimental.pallas import tpu as pltpu

D_MODEL = 1024
N_HEADS = 8
N_GROUPS = 2
HEADS_PER_GROUP = N_HEADS // N_GROUPS
HEAD_DIM = 64
ATTN_WIDTH = N_HEADS * HEAD_DIM
KV_WIDTH = N_GROUPS * HEAD_DIM
N_KV_SLOTS = 6
ROPE_DIM = HEAD_DIM // 4
ROPE_HALF = ROPE_DIM // 2
ROPE_THETA = 500000.0
CMP_BLOCK = 32
CMP_STRIDE = 16
CMP_HIDDEN = 4 * HEAD_DIM
SEL_BLOCK = 64
N_SEL_BLOCKS = 16
WINDOW = 512
CONV_WIDTH = 512
CONV_KERNEL = 3
D_FF = 2816
NORM_EPS = 1e-6
MASK_VALUE = -1e30
FORCE_SCORE = 1e4
N_GATES = 3 * N_HEADS

LANES = 128
SLAB = LANES
MXU_DIM = 256
Q_PAD = N_HEADS * SLAB
KV_PAD = N_GROUPS * SLAB
V_ROWS = 80
VT_PAD = N_GROUPS * V_ROWS
SEL_BIAS = -float(2 ** 30)
SEL_LANE0 = HEAD_DIM
Q_SCALE = HEAD_DIM ** -0.5 * math.log2(math.e)

TM = 512
SUB_TILES = 2
TB = TM * SUB_TILES
NORM_PARTS = 2
TQ = 256
FF_CHUNKS = (6 * MXU_DIM, 5 * MXU_DIM)
VMEM_LIMIT = 56 * 1024 * 1024

F32 = jnp.float32
BF16 = jnp.bfloat16


def _const_spec(shape):
    return pl.BlockSpec(shape, lambda *_: (0,) * len(shape), pipeline_mode=pl.Buffered(1))


def _params(n_axes):
    return pltpu.CompilerParams(dimension_semantics=("parallel",) * n_axes, vmem_limit_bytes=VMEM_LIMIT)


def _rms_rows(x, g):
    ms = jnp.mean(x * x, axis=-1, keepdims=True)
    return x * lax.rsqrt(ms + NORM_EPS) * g


def _norm_parts(x, g):
    part = x.shape[0] // NORM_PARTS
    parts = [_rms_rows(x[r:r + part], g).astype(BF16) for r in range(0, x.shape[0], part)]
    return parts, jnp.concatenate(parts, axis=0)


def _dot_parts(parts, w):
    return jnp.concatenate([jnp.dot(p, w, preferred_element_type=F32) for p in parts], axis=0)


def _ffn_kernel(x_ref, g_ref, wg_ref, wu_ref, wd_ref, o_ref):
    for r in range(SUB_TILES):
        rows = slice(r * TM, (r + 1) * TM)
        x = x_ref[rows, :]
        h_parts, h = _norm_parts(x, g_ref[...])
        acc = jnp.zeros(x.shape, F32)
        c0 = 0
        for width in FF_CHUNKS:
            sl = slice(c0, c0 + width)
            if c0 == 0:
                gate = _dot_parts(h_parts, wg_ref[:, sl])
            else:
                gate = jnp.dot(h, wg_ref[:, sl], preferred_element_type=F32)
            up = jnp.dot(h, wu_ref[:, sl], preferred_element_type=F32)
            act = (gate * jax.nn.sigmoid(gate) * up).astype(BF16)
            acc = acc + jnp.dot(act, wd_ref[sl, :], preferred_element_type=F32)
            c0 += width
        o_ref[rows, :] = x + 0.5 * acc


def _ffn(x, g, wg, wu, wd):
    n = x.shape[0]
    return pl.pallas_call(
        _ffn_kernel,
        grid=(n // TB,),
        in_specs=[
            pl.BlockSpec((TB, D_MODEL), lambda i: (i, 0)),
            _const_spec((1, D_MODEL)),
            _const_spec((D_MODEL, D_FF)),
            _const_spec((D_MODEL, D_FF)),
            _const_spec((D_FF, D_MODEL)),
        ],
        out_specs=pl.BlockSpec((TB, D_MODEL), lambda i: (i, 0)),
        out_shape=jax.ShapeDtypeStruct((n, D_MODEL), F32),
        compiler_params=_params(1),
        name="ffn",
    )(x, g, wg, wu, wd)


_C_Q = 0
_C_KSEL = _C_Q + Q_PAD
_C_KWIN = _C_KSEL + KV_PAD
_C_V = _C_KWIN + KV_PAD
_C_KCMP = _C_V + 2 * KV_WIDTH
_C_VCMP = _C_KCMP + KV_WIDTH
_C_GATE = _C_VCMP + KV_WIDTH
_C_CONV = _C_GATE + KV_PAD
_C_END = _C_CONV + 3 * CONV_WIDTH
N_CHUNKS_ROW = CMP_STRIDE * KV_WIDTH


def _norm_rope_slab(x, g, cos, sin_lo, sin_hi):
    ms = jnp.sum(x * x, axis=-1, keepdims=True) * (1.0 / HEAD_DIM)
    y = x * lax.rsqrt(ms + NORM_EPS) * g
    return y * cos + pltpu.roll(y, ROPE_HALF, 1) * sin_hi + pltpu.roll(y, SLAB - ROPE_HALF, 1) * sin_lo


def _inproj_kernel(x_ref, g_ref, w_ref, qg_ref, kg_ref, cos_ref, slo_ref, shi_ref, oh_ref,
                   q_ref, ksel_ref, kwin_ref, vsel_ref, vwin_ref, kcmp_ref, vcmp_ref, gate_ref, cu_ref, bg_ref,
                   stage_ref):
    for r in range(SUB_TILES):
        rows = pl.ds(r * TM, TM)
        chunk_rows = pl.ds(r * (TM // CMP_STRIDE), TM // CMP_STRIDE)
        _inproj_tile(x_ref.at[rows], g_ref, w_ref, qg_ref, kg_ref,
                     cos_ref.at[rows], slo_ref.at[rows], shi_ref.at[rows], oh_ref.at[rows],
                     q_ref.at[rows], ksel_ref.at[rows], kwin_ref.at[rows], vsel_ref.at[:, rows], vwin_ref.at[:, rows],
                     kcmp_ref.at[chunk_rows], vcmp_ref.at[chunk_rows], gate_ref.at[rows], cu_ref.at[rows],
                     bg_ref.at[rows], stage_ref.at[r])


def _inproj_tile(x_ref, g_ref, w_ref, qg_ref, kg_ref, cos_ref, slo_ref, shi_ref, oh_ref,
                 q_ref, ksel_ref, kwin_ref, vsel_ref, vwin_ref, kcmp_ref, vcmp_ref, gate_ref, cu_ref, bg_ref,
                 stage_ref):
    h_parts, h = _norm_parts(x_ref[...], g_ref[...])
    cos, slo, shi = cos_ref[...], slo_ref[...], shi_ref[...]

    def proj(c0, width):
        return jnp.dot(h, w_ref[:, c0:c0 + width], preferred_element_type=F32)

    def emit_v():
        v_t = proj(_C_V, 2 * KV_WIDTH).T
        tail = jnp.where(lax.broadcasted_iota(jnp.int32, (V_ROWS - HEAD_DIM, TM), 0) == 0, 1.0, 0.0)
        for b, v_ref in enumerate((vsel_ref, vwin_ref)):
            for g in range(N_GROUPS):
                r0 = (b * N_GROUPS + g) * HEAD_DIM
                v_ref[g * V_ROWS:(g + 1) * V_ROWS, :] = jnp.concatenate([v_t[r0:r0 + HEAD_DIM], tail],
                                                                        axis=0).astype(BF16)

    def emit_q():
        q = _dot_parts(h_parts, w_ref[:, _C_Q:_C_Q + Q_PAD])
        qg = qg_ref[...]
        for hd in range(N_HEADS):
            sl = slice(hd * SLAB, (hd + 1) * SLAB)
            y = _norm_rope_slab(q[:, sl], qg, cos, slo, shi) * Q_SCALE
            q_ref[:, sl] = y.astype(BF16)

    def emit_k():
        ksel = proj(_C_KSEL, KV_PAD)
        kwin = proj(_C_KWIN, KV_PAD)
        oh = oh_ref[...]
        off = jnp.where(lax.broadcasted_iota(jnp.int32, (1, SLAB), 1) == HEAD_DIM, SEL_BIAS, 0.0)
        for g in range(N_GROUPS):
            sl = slice(g * SLAB, (g + 1) * SLAB)
            ksel_ref[:, sl] = (_norm_rope_slab(ksel[:, sl], kg_ref[0:1, :], cos, slo, shi) + oh).astype(BF16)
            kwin_ref[:, sl] = (_norm_rope_slab(kwin[:, sl], kg_ref[1:2, :], cos, slo, shi) + off).astype(BF16)

    def emit_cmp():
        kv_cmp = proj(_C_KCMP, 2 * KV_WIDTH)
        for c, out_ref in enumerate((kcmp_ref, vcmp_ref)):
            stage_ref[c] = kv_cmp[:, c * KV_WIDTH:(c + 1) * KV_WIDTH]
            for tok in range(CMP_STRIDE):
                out_ref[:, tok * KV_WIDTH:(tok + 1) * KV_WIDTH] = stage_ref[c, pl.ds(tok, TM // CMP_STRIDE,
                                                                                     stride=CMP_STRIDE), :]

    def emit_gates():
        gate_ref[...] = jax.nn.sigmoid(proj(_C_GATE, KV_PAD))

    def emit_conv():
        conv = proj(_C_CONV, 3 * CONV_WIDTH)
        bg_ref[...] = conv[:, :CONV_WIDTH]
        cu_ref[...] = conv[:, CONV_WIDTH:2 * CONV_WIDTH] * conv[:, 2 * CONV_WIDTH:]

    for emit in (emit_q, emit_k, emit_v, emit_cmp, emit_gates, emit_conv):
        emit()


def _inproj(x1, g, w_b, qg, kg, cos, slo, shi, oh, seq):
    n = x1.shape[0]
    tiles_per_seq = seq // TB
    row = lambda i: (i, 0)
    pos = lambda i: (i % tiles_per_seq, 0)
    vt = (pl.BlockSpec((None, VT_PAD, TB), lambda i: (i // tiles_per_seq, 0, i % tiles_per_seq)),
          jax.ShapeDtypeStruct((n // seq, VT_PAD, seq), BF16))
    chunks = (pl.BlockSpec((None, TB // CMP_STRIDE, N_CHUNKS_ROW),
                           lambda i: (i // tiles_per_seq, i % tiles_per_seq, 0)),
              jax.ShapeDtypeStruct((n // seq, seq // CMP_STRIDE, N_CHUNKS_ROW), F32))
    rows = lambda width, dtype: (pl.BlockSpec((TB, width), row), jax.ShapeDtypeStruct((n, width), dtype))
    outs = [rows(Q_PAD, BF16), rows(KV_PAD, BF16), rows(KV_PAD, BF16), vt, vt, chunks, chunks,
            rows(KV_PAD, F32), rows(CONV_WIDTH, F32), rows(CONV_WIDTH, F32)]
    return pl.pallas_call(
        _inproj_kernel,
        grid=(n // TB,),
        in_specs=[
            pl.BlockSpec((TB, D_MODEL), row),
            _const_spec((1, D_MODEL)),
            _const_spec((D_MODEL, _C_END)),
            _const_spec((1, SLAB)),
            _const_spec((2, SLAB)),
            pl.BlockSpec((TB, SLAB), pos),
            pl.BlockSpec((TB, SLAB), pos),
            pl.BlockSpec((TB, SLAB), pos),
            pl.BlockSpec((TB, SLAB), pos),
        ],
        out_specs=[spec for spec, _ in outs],
        out_shape=[shape for _, shape in outs],
        scratch_shapes=[pltpu.VMEM((SUB_TILES, 2, TM, KV_WIDTH), F32)],
        compiler_params=_params(1),
        name="inproj",
    )(x1, g, w_b, qg, kg, cos, slo, shi, oh)


def _compress_kernel(kc_ref, vc_ref, pek_ref, pev_ref, wk1_ref, wv1_ref, wk2_ref, wv2_ref, kg_ref,
                     ko_ref, vo_ref):
    def branch(r_ref, pe_ref, w1_ref, w2_ref):
        r = r_ref[0]
        top = jnp.dot((r + pe_ref[0:1, :]).astype(BF16), w1_ref[0], preferred_element_type=F32)
        bot = jnp.dot((r + pe_ref[1:2, :]).astype(BF16), w1_ref[1], preferred_element_type=F32)
        hid = top + pltpu.roll(bot, bot.shape[0] - 1, 0)
        hid = (hid * jax.nn.sigmoid(hid)).astype(BF16)
        return [jnp.dot(hid[:, g * CMP_HIDDEN:(g + 1) * CMP_HIDDEN], w2_ref[...], preferred_element_type=F32)
                for g in range(N_GROUPS)]

    k = branch(kc_ref, pek_ref, wk1_ref, wk2_ref)
    v = branch(vc_ref, pev_ref, wv1_ref, wv2_ref)
    for g in range(N_GROUPS):
        ms = jnp.sum(k[g] * k[g], axis=-1, keepdims=True) * (1.0 / HEAD_DIM)
        ko_ref[0, g] = (k[g] * lax.rsqrt(ms + NORM_EPS) * kg_ref[...]).astype(BF16)
        vo_ref[0, g] = v[g].T.astype(BF16)


def _compress(kc, vc, pek, pev, wk1, wv1, wk2, wv2, kg):
    b, n_chunks, _ = kc.shape
    blk = pl.BlockSpec((1, n_chunks, N_CHUNKS_ROW), lambda i: (i, 0, 0))
    k_out = pl.BlockSpec((1, N_GROUPS, n_chunks, SLAB), lambda i: (i, 0, 0, 0))
    v_out = pl.BlockSpec((1, N_GROUPS, SLAB, n_chunks), lambda i: (i, 0, 0, 0))
    return pl.pallas_call(
        _compress_kernel,
        grid=(b,),
        in_specs=[blk, blk,
                  _const_spec((2, N_CHUNKS_ROW)), _const_spec((2, N_CHUNKS_ROW)),
                  _const_spec((2, N_CHUNKS_ROW, N_GROUPS * CMP_HIDDEN)),
                  _const_spec((2, N_CHUNKS_ROW, N_GROUPS * CMP_HIDDEN)),
                  _const_spec((CMP_HIDDEN, SLAB)), _const_spec((CMP_HIDDEN, SLAB)),
                  _const_spec((1, SLAB))],
        out_specs=[k_out, v_out],
        out_shape=[jax.ShapeDtypeStruct((b, N_GROUPS, n_chunks, SLAB), BF16),
                   jax.ShapeDtypeStruct((b, N_GROUPS, SLAB, n_chunks), BF16)],
        compiler_params=_params(1),
        name="compress",
    )(kc, vc, pek, pev, wk1, wv1, wk2, wv2, kg)


N_QL = HEADS_PER_GROUP * TQ
N_EARLY_WINDOW_TILES = 3
SCORES_AHEAD = 6

def _attn_pair(t, q_ref, kc_ref, vc_ref, ksel_ref, vsel_ref, kwin_ref, vwin_ref, gate_ref, ovl_ref, o_ref,
               qa_ref, qw_ref, m_ref, acc_ref, mw_ref, accw_ref, *, n_tiles, n_sel_blocks, k_eff, n_early):
    tile = (t, n_tiles - 1 - t)
    n_cmp = kc_ref.shape[0]

    def per_head(x):
        return jnp.concatenate([x] * HEADS_PER_GROUP, axis=1)

    def rows_of(x):
        return pl.ds(pl.multiple_of(tile[x] * TQ, TQ), TQ)

    def keys(kt):
        return pl.ds(pl.multiple_of(kt * TQ, TQ), TQ)

    q_t = [[q_ref[rows_of(x), h * SLAB:(h + 1) * SLAB].astype(F32).T[:HEAD_DIM] for h in range(HEADS_PER_GROUP)]
           for x in range(2)]

    def q_operand(x, extra_rows):
        n_pad = SLAB - HEAD_DIM - sum(r.shape[0] for r in extra_rows)
        tail = extra_rows + [jnp.zeros((n_pad, TQ), F32)]
        return jnp.concatenate([jnp.concatenate([q_t[x][h]] + tail, axis=0) for h in range(HEADS_PER_GROUP)],
                               axis=1).astype(BF16)

    q_all = [q_operand(x, []) for x in range(2)]
    qw_ref[0] = q_all[0]
    qw_ref[1] = q_operand(0, [jnp.ones((1, TQ), F32)])

    for mr, ar in ((m_ref, acc_ref), (mw_ref, accw_ref)):
        mr[...] = jnp.full(mr.shape, MASK_VALUE, F32)
        ar[...] = jnp.zeros(ar.shape, F32)

    key_r = lax.broadcasted_iota(jnp.int32, (TQ, TQ), 0)
    qry_c = lax.broadcasted_iota(jnp.int32, (TQ, TQ), 1)

    def head_step(mr, ar, x, sl, s, v_t):
        m_old = mr[x, :, sl]
        m_new = jnp.maximum(m_old, jnp.max(s, axis=0, keepdims=True))
        p = jnp.exp2(s - m_new).astype(BF16)
        ar[x, :, sl] = jnp.exp2(m_old - m_new) * ar[x, :, sl] + jnp.dot(v_t, p, preferred_element_type=F32)
        mr[x, :, sl] = m_new

    def tile_work(k_ref, v_ref, kt, q_head, mask, mr, ar, x):
        work = []
        for h in range(HEADS_PER_GROUP):
            sl = slice(h * TQ, (h + 1) * TQ)
            work.append((
                functools.partial(lambda sl: mask(jnp.dot(k_ref[keys(kt), :], q_head(sl),
                                                          preferred_element_type=F32)), sl),
                functools.partial(lambda sl, s: head_step(mr, ar, x, sl, s, v_ref[:, keys(kt)]), sl)))
        return work

    def result(ar, x):
        acc = ar[x]
        return acc * (1.0 / acc[HEAD_DIM:HEAD_DIM + 1, :])

    def run(work, ahead=SCORES_AHEAD):
        pending = [w[0]() for w in work[:ahead]]
        for i, (_, update) in enumerate(work):
            if i + ahead < len(work):
                pending.append(work[i + ahead][0]())
            update(pending.pop(0))

    diff0 = qry_c - key_r
    no_mask = lambda s: s
    causal = lambda s: jnp.where(diff0 >= 0, s, MASK_VALUE)
    band = lambda s: jnp.where(diff0 < 0, s, MASK_VALUE)

    win_work = []
    for x in range(2):
        for back in range(WINDOW // TQ, -1, -1):
            kt_raw = tile[x] - back
            if x == 0 and back > 0:
                off = jnp.where(kt_raw >= 0, 0, 1)
                q_head = functools.partial(lambda off, sl: qw_ref[off, :, sl], off)
            else:
                q_head = functools.partial(lambda x, sl: q_all[x][:, sl], x)
            mask = causal if back == 0 else band if back * TQ == WINDOW else no_mask
            win_work.append(tile_work(kwin_ref, vwin_ref, jnp.maximum(kt_raw, 0), q_head, mask, mw_ref, accw_ref, x))

    flat = lambda tiles: [pair for tile_pairs in tiles for pair in tile_pairs]
    run(flat(win_work[:n_early]))

    lane = lax.broadcasted_iota(jnp.int32, (n_cmp, TQ), 1)
    cmp_end = lax.broadcasted_iota(jnp.int32, (n_cmp, TQ), 0) * CMP_STRIDE + (CMP_BLOCK - 1)
    valid_c = jnp.concatenate([per_head(cmp_end <= tile[x] * TQ + lane) for x in range(2)], axis=1)
    s = jnp.dot(kc_ref[...], jnp.concatenate(q_all, axis=1), preferred_element_type=F32)
    s = jnp.where(valid_c, s, MASK_VALUE)
    e = jnp.exp2(s - jnp.max(s, axis=0, keepdims=True))
    p = jnp.where(valid_c, e * (1.0 / jnp.sum(e, axis=0, keepdims=True)), 0.0)
    o_cmp = jnp.dot(vc_ref[...], p.astype(BF16), preferred_element_type=F32)
    qa_ref[0] = q_all[0]
    p_sum = p[:, N_QL:N_QL + TQ]
    for h in range(1, HEADS_PER_GROUP):
        p_sum = p_sum + p[:, N_QL + h * TQ:N_QL + (h + 1) * TQ]
    p_hi = p_sum.astype(BF16)
    p_lo = (p_sum - p_hi.astype(F32)).astype(BF16)
    ovl = ovl_ref[...]
    imp = (jnp.dot(ovl, p_hi, preferred_element_type=F32)
           + jnp.dot(ovl, p_lo, preferred_element_type=F32))
    j_idx = lax.broadcasted_iota(jnp.int32, (n_sel_blocks, TQ), 0)
    pos_t = tile[1] * TQ + lax.broadcasted_iota(jnp.int32, (n_sel_blocks, TQ), 1)
    cur = pos_t // SEL_BLOCK
    imp = jnp.where(j_idx * SEL_BLOCK > pos_t, -1.0, imp)
    imp = jnp.where(j_idx == 0, FORCE_SCORE, imp)
    imp = jnp.where(j_idx == cur, FORCE_SCORE, imp)
    imp = jnp.where(j_idx == cur - 1, FORCE_SCORE, imp)
    grp = 8
    j_loc = lax.broadcasted_iota(jnp.int32, (grp, TQ), 0)
    bias = []
    for g0 in range(0, n_sel_blocks, grp):
        imp_g = imp[g0:g0 + grp]
        rank = jnp.zeros((grp, TQ), F32)
        for jp in range(n_sel_blocks):
            row = imp[jp:jp + 1, :]
            if jp >= g0 + grp - 1:
                rank = jnp.where(row > imp_g, rank + 1.0, rank)
            elif jp < g0:
                rank = jnp.where(row >= imp_g, rank + 1.0, rank)
            else:
                tie = jnp.where(j_loc > jp - g0, 1.0, 0.0)
                rank = rank + jnp.where(row > imp_g, 1.0, jnp.where(row == imp_g, tie, 0.0))
        bias.append(jnp.where(rank < k_eff, 0.0, SEL_BIAS))
    qa_ref[1] = q_operand(1, bias)

    sel_work = []
    for i in range(n_tiles - 1):
        x = jnp.where(i >= t, 1, 0)
        kt = jnp.where(i >= t, i - t, i)
        sel_work.append(tile_work(ksel_ref, vsel_ref, kt, functools.partial(lambda x, sl: qa_ref[x, :, sl], x),
                                  no_mask, m_ref, acc_ref, x))
    for x in range(2):
        sel_work.append(tile_work(ksel_ref, vsel_ref, tile[x], functools.partial(lambda x, sl: qa_ref[x, :, sl], x),
                                  causal, m_ref, acc_ref, x))

    rest = win_work[n_early:]
    work = []
    for i in range(max(len(sel_work), len(rest))):
        if i < len(rest):
            work += [pair for both in zip(sel_work[i], rest[i]) for pair in both]
        else:
            work += sel_work[i]
    run(work)

    for x in range(2):
        gates_t = gate_ref[rows_of(x), :].T
        o_sel = result(acc_ref, x)
        o_win = result(accw_ref, x)
        outs = []
        for h in range(HEADS_PER_GROUP):
            sl = slice(h * TQ, (h + 1) * TQ)
            o = (gates_t[h:h + 1] * o_cmp[:HEAD_DIM, x * N_QL + h * TQ:x * N_QL + (h + 1) * TQ]
                 + gates_t[HEADS_PER_GROUP + h:HEADS_PER_GROUP + h + 1] * o_sel[:HEAD_DIM, sl]
                 + gates_t[2 * HEADS_PER_GROUP + h:2 * HEADS_PER_GROUP + h + 1] * o_win[:HEAD_DIM, sl])
            outs.append(o)
        for hp in range(HEADS_PER_GROUP // 2):
            pair = jnp.concatenate(outs[2 * hp:2 * hp + 2], axis=0).T
            o_ref[rows_of(x), hp * SLAB:(hp + 1) * SLAB] = pair.astype(BF16)


def _attn_kernel(*refs, n_tiles, **static):
    @pl.loop(0, n_tiles // 2)
    def _(t):
        _attn_pair(t, *refs, n_tiles=n_tiles, **static)


def _attention(q, kc, vc, ksel, vsel, kwin, vwin, gates, ovl, batch, seq):
    n = q.shape[0]
    n_cmp = kc.shape[2]
    n_sel_blocks = seq // SEL_BLOCK
    n_tiles = seq // TQ
    width = HEADS_PER_GROUP * HEAD_DIM
    bg = lambda b, g: (b, g)
    return pl.pallas_call(
        functools.partial(_attn_kernel, n_tiles=n_tiles, n_sel_blocks=n_sel_blocks,
                          k_eff=min(N_SEL_BLOCKS, n_sel_blocks), n_early=N_EARLY_WINDOW_TILES),
        grid=(batch, N_GROUPS),
        in_specs=[
            pl.BlockSpec((seq, HEADS_PER_GROUP * SLAB), bg),
            pl.BlockSpec((None, None, n_cmp, SLAB), lambda b, g: (b, g, 0, 0)),
            pl.BlockSpec((None, None, SLAB, n_cmp), lambda b, g: (b, g, 0, 0)),
            pl.BlockSpec((seq, SLAB), bg),
            pl.BlockSpec((None, V_ROWS, seq), lambda b, g: (b, g, 0)),
            pl.BlockSpec((seq, SLAB), bg),
            pl.BlockSpec((None, V_ROWS, seq), lambda b, g: (b, g, 0)),
            pl.BlockSpec((seq, SLAB), bg),
            _const_spec((n_sel_blocks, n_cmp)),
        ],
        out_specs=pl.BlockSpec((seq, width), bg),
        out_shape=jax.ShapeDtypeStruct((n, N_GROUPS * width), BF16),
        scratch_shapes=[pltpu.VMEM((2, SLAB, N_QL), BF16), pltpu.VMEM((2, SLAB, N_QL), BF16),
                        pltpu.VMEM((2, 1, N_QL), F32), pltpu.VMEM((2, V_ROWS, N_QL), F32),
                        pltpu.VMEM((2, 1, N_QL), F32), pltpu.VMEM((2, V_ROWS, N_QL), F32)],
        compiler_params=_params(2),
        name="attention",
    )(q, kc, vc, ksel, vsel, kwin, vwin, gates, ovl)


HALO = 8


def _merge_kernel(x_ref, a_ref, cu_ref, halo_ref, bg_ref, g_ref, wm_ref, wa_ref, wc_ref, wo_ref, cw_ref, o_ref,
                  *, blocks_per_seq):
    for r in range(SUB_TILES):
        rows = pl.ds(r * TM, TM)
        x = x_ref[rows, :]
        h_parts, _ = _norm_parts(x, g_ref[...])
        mg = jax.nn.sigmoid(_dot_parts(h_parts, wm_ref[...]))
        att = jnp.dot(a_ref[rows, :], wa_ref[...], preferred_element_type=F32)

        cu = cu_ref[rows, :]
        if r == 0:
            first = pl.program_id(0) % blocks_per_seq == 0
            halo = jnp.where(first, 0.0, halo_ref[...])
        else:
            halo = cu_ref[pl.ds(r * TM - HALO, HALO), :]
        row = lax.broadcasted_iota(jnp.int32, cu.shape, 0)
        prev1 = jnp.where(row == 0, halo[HALO - 1:HALO, :], pltpu.roll(cu, 1, 0))
        prev2 = jnp.where(row == 0, halo[HALO - 2:HALO - 1, :],
                          jnp.where(row == 1, halo[HALO - 1:HALO, :], pltpu.roll(cu, 2, 0)))
        conv = cw_ref[0:1, :] * prev2 + cw_ref[1:2, :] * prev1 + cw_ref[2:3, :] * cu
        c = (bg_ref[rows, :] * conv).astype(BF16)
        cb = jnp.dot(c, wc_ref[...], preferred_element_type=F32)

        merged = (mg[:, :D_MODEL] * att + mg[:, D_MODEL:] * cb).astype(BF16)
        o_ref[rows, :] = x + jnp.dot(merged, wo_ref[...], preferred_element_type=F32)


def _merge(x1, a, cu, bg, g, wm, wa, wc, wo, cw, seq):
    n = x1.shape[0]
    row = lambda i: (i, 0)
    return pl.pallas_call(
        functools.partial(_merge_kernel, blocks_per_seq=seq // TB),
        grid=(n // TB,),
        in_specs=[
            pl.BlockSpec((TB, D_MODEL), row),
            pl.BlockSpec((TB, ATTN_WIDTH), row),
            pl.BlockSpec((TB, CONV_WIDTH), row),
            pl.BlockSpec((HALO, CONV_WIDTH), lambda i: (jnp.maximum(i * (TB // HALO) - 1, 0), 0)),
            pl.BlockSpec((TB, CONV_WIDTH), row),
            _const_spec((1, D_MODEL)),
            _const_spec((D_MODEL, 2 * D_MODEL)),
            _const_spec((ATTN_WIDTH, D_MODEL)),
            _const_spec((CONV_WIDTH, D_MODEL)),
            _const_spec((D_MODEL, D_MODEL)),
            _const_spec((CONV_KERNEL, CONV_WIDTH)),
        ],
        out_specs=pl.BlockSpec((TB, D_MODEL), row),
        out_shape=jax.ShapeDtypeStruct((n, D_MODEL), F32),
        compiler_params=_params(1),
        name="merge",
    )(x1, a, cu, cu, bg, g, wm, wa, wc, wo, cw)


def _pad_heads(w, n_heads):
    rows = w.shape[0]
    w = w.reshape(rows, n_heads, HEAD_DIM)
    return jnp.pad(w, ((0, 0), (0, 0), (0, SLAB - HEAD_DIM))).reshape(rows, n_heads * SLAB)


def _pad_lanes(v):
    return jnp.pad(v, ((0, 0), (0, SLAB - v.shape[-1])))


def _rope_tables(seq):
    inv_freq = ROPE_THETA ** (-jnp.arange(0, ROPE_DIM, 2, dtype=F32) / ROPE_DIM)
    ang = jnp.arange(seq, dtype=F32)[:, None] * inv_freq[None, :]
    cos, sin = jnp.cos(ang), jnp.sin(ang)
    zeros = jnp.zeros((seq, SLAB - ROPE_DIM), F32)
    half0 = jnp.zeros((seq, ROPE_HALF), F32)
    cos_t = jnp.concatenate([cos, cos, jnp.ones((seq, SLAB - ROPE_DIM), F32)], axis=1)
    sin_lo = jnp.concatenate([-sin, half0, zeros], axis=1)
    sin_hi = jnp.concatenate([half0, sin, zeros], axis=1)
    return cos_t, sin_lo, sin_hi


def _block_onehot(seq):
    blk = jnp.arange(seq, dtype=jnp.int32) // SEL_BLOCK
    return (jnp.arange(SLAB, dtype=jnp.int32)[None, :] == (blk[:, None] + SEL_LANE0)).astype(F32)


def _overlap_t(seq, n_cmp):
    n = np.arange(n_cmp)
    j = np.arange(seq // SEL_BLOCK)
    cs, ce = n * CMP_STRIDE, n * CMP_STRIDE + CMP_BLOCK
    ss, se = j * SEL_BLOCK, j * SEL_BLOCK + SEL_BLOCK
    real = n < (seq // CMP_STRIDE - CMP_BLOCK // CMP_STRIDE + 1)
    ovl = (cs[None, :] < se[:, None]) & (ce[None, :] > ss[:, None]) & real[None, :]
    return jnp.asarray(ovl.astype(np.float32), dtype=BF16)


def _expand_cmp_w1(w1):
    half = CMP_BLOCK // CMP_STRIDE
    w = w1.astype(BF16).reshape(half, CMP_STRIDE, 1, HEAD_DIM, 1, CMP_HIDDEN)
    same_group = jnp.eye(N_GROUPS, dtype=bool).reshape(1, 1, N_GROUPS, 1, N_GROUPS, 1)
    big = jnp.where(same_group, w, jnp.zeros((), BF16))
    return big.reshape(half, N_CHUNKS_ROW, N_GROUPS * CMP_HIDDEN)


def _expand_pe(pe):
    half = CMP_BLOCK // CMP_STRIDE
    p = pe.reshape(half, CMP_STRIDE, 1, HEAD_DIM)
    return jnp.broadcast_to(p, (half, CMP_STRIDE, N_GROUPS, HEAD_DIM)).reshape(half, N_CHUNKS_ROW)


def kernel(x, ffn1_norm_g, ffn1_w_gate, ffn1_w_up, ffn1_w_down, mix_norm_g, w_in, q_norm_g, k_norm_g, cmp_pe_k, cmp_pe_v, cmp_k_w1, cmp_k_w2, cmp_v_w1, cmp_v_w2, conv_w, w_attn_branch, w_conv_branch, w_out, ffn2_norm_g, ffn2_w_gate, ffn2_w_up, ffn2_w_down):
    batch, seq, d_model = x.shape
    depth = ffn1_norm_g.shape[0]
    n_chunks = seq // CMP_STRIDE
    assert d_model == D_MODEL and seq % TB == 0 and seq % (2 * TQ) == 0 and sum(FF_CHUNKS) == D_FF
    assert seq // SEL_BLOCK + SEL_LANE0 <= SLAB and n_chunks % SLAB == 0 and WINDOW % TQ == 0
    assert seq // 2 <= min(N_SEL_BLOCKS, seq // SEL_BLOCK) * SEL_BLOCK
    cos_t, sin_lo, sin_hi = _rope_tables(seq)
    onehot = _block_onehot(seq)
    ovl = _overlap_t(seq, n_chunks)
    xs = x.reshape(batch * seq, D_MODEL)

    for l in range(depth):
        xs = _ffn(xs, ffn1_norm_g[l][None], ffn1_w_gate[l].astype(BF16), ffn1_w_up[l].astype(BF16),
                  ffn1_w_down[l].astype(BF16))

        w = w_in[l].astype(BF16)
        c_kv = ATTN_WIDTH
        c_gate = c_kv + N_KV_SLOTS * KV_WIDTH
        c_conv = c_gate + N_GATES
        c_merge = c_conv + 3 * CONV_WIDTH
        kv = [w[:, c_kv + s * KV_WIDTH:c_kv + (s + 1) * KV_WIDTH] for s in range(N_KV_SLOTS)]
        wgate = w[:, c_gate:c_conv].reshape(D_MODEL, 3, N_GROUPS, HEADS_PER_GROUP).transpose(0, 2, 1, 3)
        wgate = jnp.pad(wgate.reshape(D_MODEL, N_GROUPS, 3 * HEADS_PER_GROUP),
                        ((0, 0), (0, 0), (0, SLAB - 3 * HEADS_PER_GROUP))).reshape(D_MODEL, KV_PAD)
        w_b = jnp.concatenate([
            _pad_heads(w[:, :ATTN_WIDTH], N_HEADS),
            _pad_heads(kv[2], N_GROUPS), _pad_heads(kv[4], N_GROUPS), kv[3], kv[5],
            kv[0], kv[1], wgate, w[:, c_conv:c_merge]], axis=1)
        q, ksel, kwin, vsel, vwin, kcmp, vcmp, gates, cu, bg = _inproj(
            xs, mix_norm_g[l][None], w_b, _pad_lanes(q_norm_g[l][None]), _pad_lanes(k_norm_g[l][1:3]),
            cos_t, sin_lo, sin_hi, onehot, seq)

        w2k = _pad_lanes(cmp_k_w2[l]).astype(BF16)
        w2v = _pad_lanes(cmp_v_w2[l]).astype(BF16)
        kc, vc = _compress(kcmp, vcmp, _expand_pe(cmp_pe_k[l]), _expand_pe(cmp_pe_v[l]),
                           _expand_cmp_w1(cmp_k_w1[l]), _expand_cmp_w1(cmp_v_w1[l]), w2k, w2v,
                           _pad_lanes(k_norm_g[l][0:1]))

        a = _attention(q, kc, vc, ksel, vsel, kwin, vwin, gates, ovl, batch, seq)

        xs = _merge(xs, a, cu, bg, mix_norm_g[l][None], w[:, c_merge:], w_attn_branch[l].astype(BF16),
                    w_conv_branch[l].astype(BF16), w_out[l].astype(BF16), conv_w[l], seq)

        xs = _ffn(xs, ffn2_norm_g[l][None], ffn2_w_gate[l].astype(BF16), ffn2_w_up[l].astype(BF16),
                  ffn2_w_down[l].astype(BF16))
    return xs.reshape(batch, seq, D_MODEL)
```

```python
import functools
import math

import numpy as np
import jax
import jax.numpy as jnp
from jax import lax
from jax.experimental import pallas as pl
from jax.experimental.pallas import tpu as pltpu

D_MODEL = 1024
N_HEADS = 8
N_GROUPS = 2
HEADS_PER_GROUP = N_HEADS // N_GROUPS
HEAD_DIM = 64
ATTN_WIDTH = N_HEADS * HEAD_DIM
KV_WIDTH = N_GROUPS * HEAD_DIM
N_KV_SLOTS = 6
ROPE_DIM = HEAD_DIM // 4
ROPE_HALF = ROPE_DIM // 2
ROPE_THETA = 500000.0
CMP_BLOCK = 32
CMP_STRIDE = 16
CMP_HIDDEN = 4 * HEAD_DIM
SEL_BLOCK = 64
N_SEL_BLOCKS = 16
WINDOW = 512
CONV_WIDTH = 512
CONV_KERNEL = 3
D_FF = 2816
NORM_EPS = 1e-6
MASK_VALUE = -1e30
FORCE_SCORE = 1e4
N_GATES = 3 * N_HEADS

LANES = 128
SLAB = LANES
MXU_DIM = 256
Q_PAD = N_HEADS * SLAB
KV_PAD = N_GROUPS * SLAB
V_ROWS = 80
VT_PAD = N_GROUPS * V_ROWS
SEL_BIAS = -float(2 ** 30)
SEL_LANE0 = HEAD_DIM
GATE_LANE0 = SLAB - 32
Q_SCALE = HEAD_DIM ** -0.5 * math.log2(math.e)

TM = 512
SUB_TILES = 2
TB = TM * SUB_TILES
INPROJ_TILES = 1
TBI = TM * INPROJ_TILES
NORM_PARTS = 2
TQ = 256
FF_CHUNKS = (6 * MXU_DIM, 5 * MXU_DIM)
VMEM_LIMIT = 56 * 1024 * 1024

F32 = jnp.float32
BF16 = jnp.bfloat16


def _const_spec(shape):
    return pl.BlockSpec(shape, lambda *_: (0,) * len(shape), pipeline_mode=pl.Buffered(1))


def _params(n_axes):
    return pltpu.CompilerParams(dimension_semantics=("parallel",) * n_axes, vmem_limit_bytes=VMEM_LIMIT)


def _rms_rows(x, g):
    ms = jnp.mean(x * x, axis=-1, keepdims=True)
    return x * lax.rsqrt(ms + NORM_EPS) * g


def _norm_parts(x, g):
    part = x.shape[0] // NORM_PARTS
    parts = [_rms_rows(x[r:r + part], g).astype(BF16) for r in range(0, x.shape[0], part)]
    return parts, jnp.concatenate(parts, axis=0)


def _dot_parts(parts, w):
    return jnp.concatenate([jnp.dot(p, w, preferred_element_type=F32) for p in parts], axis=0)


def _ffn_kernel(x_ref, g_ref, wg_ref, wu_ref, wd_ref, o_ref):
    for r in range(SUB_TILES):
        rows = slice(r * TM, (r + 1) * TM)
        x = x_ref[rows, :]
        h_parts, h = _norm_parts(x, g_ref[...])
        acc = jnp.zeros(x.shape, F32)
        c0 = 0
        for width in FF_CHUNKS:
            sl = slice(c0, c0 + width)
            if c0 == 0:
                gate = _dot_parts(h_parts, wg_ref[:, sl])
            else:
                gate = jnp.dot(h, wg_ref[:, sl], preferred_element_type=F32)
            up = jnp.dot(h, wu_ref[:, sl], preferred_element_type=F32)
            act = (gate * jax.nn.sigmoid(gate) * up).astype(BF16)
            acc = acc + jnp.dot(act, wd_ref[sl, :], preferred_element_type=F32)
            c0 += width
        o_ref[rows, :] = x + 0.5 * acc


def _ffn(x, g, wg, wu, wd):
    n = x.shape[0]
    return pl.pallas_call(
        _ffn_kernel,
        grid=(n // TB,),
        in_specs=[
            pl.BlockSpec((TB, D_MODEL), lambda i: (i, 0)),
            _const_spec((1, D_MODEL)),
            _const_spec((D_MODEL, D_FF)),
            _const_spec((D_MODEL, D_FF)),
            _const_spec((D_FF, D_MODEL)),
        ],
        out_specs=pl.BlockSpec((TB, D_MODEL), lambda i: (i, 0)),
        out_shape=jax.ShapeDtypeStruct((n, D_MODEL), F32),
        compiler_params=_params(1),
        name="ffn",
    )(x, g, wg, wu, wd)


_C_Q = 0
_C_KSEL = _C_Q + Q_PAD
_C_KWIN = _C_KSEL + KV_PAD
_C_V = _C_KWIN + KV_PAD
_C_KCMP = _C_V + 2 * KV_WIDTH
_C_VCMP = _C_KCMP + KV_WIDTH
_C_CONV = _C_VCMP + KV_WIDTH
_C_END = _C_CONV + 3 * CONV_WIDTH
N_CHUNKS_ROW = CMP_STRIDE * KV_WIDTH


def _norm_rope_slab(x, g, cos, sin_lo, sin_hi):
    ms = jnp.sum(x * x, axis=-1, keepdims=True) * (1.0 / HEAD_DIM)
    y = x * lax.rsqrt(ms + NORM_EPS) * g
    return y * cos + pltpu.roll(y, ROPE_HALF, 1) * sin_hi + pltpu.roll(y, SLAB - ROPE_HALF, 1) * sin_lo


def _inproj_kernel(x_ref, g_ref, w_ref, qg_ref, kg_ref, cos_ref, slo_ref, shi_ref, oh_ref,
                   q_ref, ksel_ref, kwin_ref, vsel_ref, vwin_ref, kcmp_ref, vcmp_ref, gate_ref, cu_ref, bg_ref,
                   stage_ref):
    for r in range(INPROJ_TILES):
        rows = pl.ds(r * TM, TM)
        chunk_rows = pl.ds(r * (TM // CMP_STRIDE), TM // CMP_STRIDE)
        _inproj_tile(x_ref.at[rows], g_ref, w_ref, qg_ref, kg_ref,
                     cos_ref.at[rows], slo_ref.at[rows], shi_ref.at[rows], oh_ref.at[rows],
                     q_ref.at[rows], ksel_ref.at[rows], kwin_ref.at[rows], vsel_ref.at[:, rows], vwin_ref.at[:, rows],
                     kcmp_ref.at[chunk_rows], vcmp_ref.at[chunk_rows], gate_ref.at[rows], cu_ref.at[rows],
                     bg_ref.at[rows], stage_ref.at[r])


def _inproj_tile(x_ref, g_ref, w_ref, qg_ref, kg_ref, cos_ref, slo_ref, shi_ref, oh_ref,
                 q_ref, ksel_ref, kwin_ref, vsel_ref, vwin_ref, kcmp_ref, vcmp_ref, gate_ref, cu_ref, bg_ref,
                 stage_ref):
    h_parts, h = _norm_parts(x_ref[...], g_ref[...])
    cos, slo, shi = cos_ref[...], slo_ref[...], shi_ref[...]

    def proj(c0, width):
        return jnp.dot(h, w_ref[:, c0:c0 + width], preferred_element_type=F32)

    def emit_v():
        v_t = proj(_C_V, 2 * KV_WIDTH).T
        tail = jnp.where(lax.broadcasted_iota(jnp.int32, (V_ROWS - HEAD_DIM, TM), 0) == 0, 1.0, 0.0)
        for b, v_ref in enumerate((vsel_ref, vwin_ref)):
            for g in range(N_GROUPS):
                r0 = (b * N_GROUPS + g) * HEAD_DIM
                v_ref[g * V_ROWS:(g + 1) * V_ROWS, :] = jnp.concatenate([v_t[r0:r0 + HEAD_DIM], tail],
                                                                        axis=0).astype(BF16)

    def emit_q():
        q = _dot_parts(h_parts, w_ref[:, _C_Q:_C_Q + Q_PAD])
        qg = qg_ref[...]
        for hd in range(N_HEADS):
            sl = slice(hd * SLAB, (hd + 1) * SLAB)
            y = _norm_rope_slab(q[:, sl], qg, cos, slo, shi) * Q_SCALE
            q_ref[:, sl] = y.astype(BF16)

    def emit_k():
        ksel = proj(_C_KSEL, KV_PAD)
        kwin = proj(_C_KWIN, KV_PAD)
        oh = oh_ref[...]
        off = jnp.where(lax.broadcasted_iota(jnp.int32, (1, SLAB), 1) == HEAD_DIM, SEL_BIAS, 0.0)
        gate_ref[...] = jax.nn.sigmoid(ksel)
        head_lanes = lax.broadcasted_iota(jnp.int32, (1, SLAB), 1) < HEAD_DIM
        for g in range(N_GROUPS):
            sl = slice(g * SLAB, (g + 1) * SLAB)
            k_g = jnp.where(head_lanes, ksel[:, sl], 0.0)
            ksel_ref[:, sl] = (_norm_rope_slab(k_g, kg_ref[0:1, :], cos, slo, shi) + oh).astype(BF16)
            kwin_ref[:, sl] = (_norm_rope_slab(kwin[:, sl], kg_ref[1:2, :], cos, slo, shi) + off).astype(BF16)

    def emit_cmp():
        kv_cmp = proj(_C_KCMP, 2 * KV_WIDTH)
        for c, out_ref in enumerate((kcmp_ref, vcmp_ref)):
            stage_ref[c] = kv_cmp[:, c * KV_WIDTH:(c + 1) * KV_WIDTH]
            for tok in range(CMP_STRIDE):
                out_ref[:, tok * KV_WIDTH:(tok + 1) * KV_WIDTH] = stage_ref[c, pl.ds(tok, TM // CMP_STRIDE,
                                                                                     stride=CMP_STRIDE), :]

    def emit_conv():
        conv = proj(_C_CONV, 3 * CONV_WIDTH)
        bg_ref[...] = conv[:, :CONV_WIDTH]
        cu_ref[...] = conv[:, CONV_WIDTH:2 * CONV_WIDTH] * conv[:, 2 * CONV_WIDTH:]

    for emit in (emit_q, emit_k, emit_v, emit_cmp, emit_conv):
        emit()


def _inproj(x1, g, w_b, qg, kg, cos, slo, shi, oh, seq):
    n = x1.shape[0]
    tiles_per_seq = seq // TBI
    row = lambda i: (i, 0)
    pos = lambda i: (i % tiles_per_seq, 0)
    vt = (pl.BlockSpec((None, VT_PAD, TBI), lambda i: (i // tiles_per_seq, 0, i % tiles_per_seq)),
          jax.ShapeDtypeStruct((n // seq, VT_PAD, seq), BF16))
    chunks = (pl.BlockSpec((None, TBI // CMP_STRIDE, N_CHUNKS_ROW),
                           lambda i: (i // tiles_per_seq, i % tiles_per_seq, 0)),
              jax.ShapeDtypeStruct((n // seq, seq // CMP_STRIDE, N_CHUNKS_ROW), F32))
    rows = lambda width, dtype: (pl.BlockSpec((TBI, width), row), jax.ShapeDtypeStruct((n, width), dtype))
    outs = [rows(Q_PAD, BF16), rows(KV_PAD, BF16), rows(KV_PAD, BF16), vt, vt, chunks, chunks,
            rows(KV_PAD, F32), rows(CONV_WIDTH, F32), rows(CONV_WIDTH, F32)]
    return pl.pallas_call(
        _inproj_kernel,
        grid=(n // TBI,),
        in_specs=[
            pl.BlockSpec((TBI, D_MODEL), row),
            _const_spec((1, D_MODEL)),
            _const_spec((D_MODEL, _C_END)),
            _const_spec((1, SLAB)),
            _const_spec((2, SLAB)),
            pl.BlockSpec((TBI, SLAB), pos),
            pl.BlockSpec((TBI, SLAB), pos),
            pl.BlockSpec((TBI, SLAB), pos),
            pl.BlockSpec((TBI, SLAB), pos),
        ],
        out_specs=[spec for spec, _ in outs],
        out_shape=[shape for _, shape in outs],
        scratch_shapes=[pltpu.VMEM((INPROJ_TILES, 2, TM, KV_WIDTH), F32)],
        compiler_params=_params(1),
        name="inproj",
    )(x1, g, w_b, qg, kg, cos, slo, shi, oh)


def _compress_kernel(kc_ref, vc_ref, pek_ref, pev_ref, wk1_ref, wv1_ref, wk2_ref, wv2_ref, kg_ref,
                     ko_ref, vo_ref):
    def branch(r_ref, pe_ref, w1_ref, w2_ref):
        r = r_ref[0]
        top = jnp.dot((r + pe_ref[0:1, :]).astype(BF16), w1_ref[0], preferred_element_type=F32)
        bot = jnp.dot((r + pe_ref[1:2, :]).astype(BF16), w1_ref[1], preferred_element_type=F32)
        hid = top + pltpu.roll(bot, bot.shape[0] - 1, 0)
        hid = (hid * jax.nn.sigmoid(hid)).astype(BF16)
        return [jnp.dot(hid[:, g * CMP_HIDDEN:(g + 1) * CMP_HIDDEN], w2_ref[...], preferred_element_type=F32)
                for g in range(N_GROUPS)]

    k = branch(kc_ref, pek_ref, wk1_ref, wk2_ref)
    v = branch(vc_ref, pev_ref, wv1_ref, wv2_ref)
    for g in range(N_GROUPS):
        ms = jnp.sum(k[g] * k[g], axis=-1, keepdims=True) * (1.0 / HEAD_DIM)
        ko_ref[0, g] = (k[g] * lax.rsqrt(ms + NORM_EPS) * kg_ref[...]).astype(BF16)
        vo_ref[0, g] = v[g].T.astype(BF16)


def _compress(kc, vc, pek, pev, wk1, wv1, wk2, wv2, kg):
    b, n_chunks, _ = kc.shape
    blk = pl.BlockSpec((1, n_chunks, N_CHUNKS_ROW), lambda i: (i, 0, 0))
    k_out = pl.BlockSpec((1, N_GROUPS, n_chunks, SLAB), lambda i: (i, 0, 0, 0))
    v_out = pl.BlockSpec((1, N_GROUPS, SLAB, n_chunks), lambda i: (i, 0, 0, 0))
    return pl.pallas_call(
        _compress_kernel,
        grid=(b,),
        in_specs=[blk, blk,
                  _const_spec((2, N_CHUNKS_ROW)), _const_spec((2, N_CHUNKS_ROW)),
                  _const_spec((2, N_CHUNKS_ROW, N_GROUPS * CMP_HIDDEN)),
                  _const_spec((2, N_CHUNKS_ROW, N_GROUPS * CMP_HIDDEN)),
                  _const_spec((CMP_HIDDEN, SLAB)), _const_spec((CMP_HIDDEN, SLAB)),
                  _const_spec((1, SLAB))],
        out_specs=[k_out, v_out],
        out_shape=[jax.ShapeDtypeStruct((b, N_GROUPS, n_chunks, SLAB), BF16),
                   jax.ShapeDtypeStruct((b, N_GROUPS, SLAB, n_chunks), BF16)],
        compiler_params=_params(1),
        name="compress",
    )(kc, vc, pek, pev, wk1, wv1, wk2, wv2, kg)


N_QL = HEADS_PER_GROUP * TQ
N_EARLY_WINDOW_TILES = 3
SCORES_AHEAD = 6

def _attn_pair(t, q_ref, kc_ref, vc_ref, ksel_ref, vsel_ref, kwin_ref, vwin_ref, gate_ref, ovl_ref, o_ref,
               qa_ref, qw_ref, m_ref, acc_ref, mw_ref, accw_ref, *, n_tiles, n_sel_blocks, k_eff, n_early):
    tile = (t, n_tiles - 1 - t)
    n_cmp = kc_ref.shape[0]

    def per_head(x):
        return jnp.concatenate([x] * HEADS_PER_GROUP, axis=1)

    def rows_of(x):
        return pl.ds(pl.multiple_of(tile[x] * TQ, TQ), TQ)

    def keys(kt):
        return pl.ds(pl.multiple_of(kt * TQ, TQ), TQ)

    q_t = [[q_ref[rows_of(x), h * SLAB:(h + 1) * SLAB].astype(F32).T[:HEAD_DIM] for h in range(HEADS_PER_GROUP)]
           for x in range(2)]

    def q_operand(x, extra_rows):
        n_pad = SLAB - HEAD_DIM - sum(r.shape[0] for r in extra_rows)
        tail = extra_rows + [jnp.zeros((n_pad, TQ), F32)]
        return jnp.concatenate([jnp.concatenate([q_t[x][h]] + tail, axis=0) for h in range(HEADS_PER_GROUP)],
                               axis=1).astype(BF16)

    q_all = [q_operand(x, []) for x in range(2)]
    qw_ref[0] = q_all[0]
    qw_ref[1] = q_operand(0, [jnp.ones((1, TQ), F32)])

    for mr, ar in ((m_ref, acc_ref), (mw_ref, accw_ref)):
        mr[...] = jnp.full(mr.shape, MASK_VALUE, F32)
        ar[...] = jnp.zeros(ar.shape, F32)

    key_r = lax.broadcasted_iota(jnp.int32, (TQ, TQ), 0)
    qry_c = lax.broadcasted_iota(jnp.int32, (TQ, TQ), 1)

    def head_step(mr, ar, x, sl, s, v_t):
        m_old = mr[x, :, sl]
        m_new = jnp.maximum(m_old, jnp.max(s, axis=0, keepdims=True))
        p = jnp.exp2(s - m_new).astype(BF16)
        ar[x, :, sl] = jnp.exp2(m_old - m_new) * ar[x, :, sl] + jnp.dot(v_t, p, preferred_element_type=F32)
        mr[x, :, sl] = m_new

    def tile_work(k_ref, v_ref, kt, q_head, mask, mr, ar, x):
        work = []
        for h in range(HEADS_PER_GROUP):
            sl = slice(h * TQ, (h + 1) * TQ)
            work.append((
                functools.partial(lambda sl: mask(jnp.dot(k_ref[keys(kt), :], q_head(sl),
                                                          preferred_element_type=F32)), sl),
                functools.partial(lambda sl, s: head_step(mr, ar, x, sl, s, v_ref[:, keys(kt)]), sl)))
        return work

    def result(ar, x):
        acc = ar[x]
        return acc * (1.0 / acc[HEAD_DIM:HEAD_DIM + 1, :])

    def run(work, ahead=SCORES_AHEAD):
        pending = [w[0]() for w in work[:ahead]]
        for i, (_, update) in enumerate(work):
            if i + ahead < len(work):
                pending.append(work[i + ahead][0]())
            update(pending.pop(0))

    diff0 = qry_c - key_r
    no_mask = lambda s: s
    causal = lambda s: jnp.where(diff0 >= 0, s, MASK_VALUE)
    band = lambda s: jnp.where(diff0 < 0, s, MASK_VALUE)

    win_work = []
    for x in range(2):
        for back in range(WINDOW // TQ, -1, -1):
            kt_raw = tile[x] - back
            if x == 0 and back > 0:
                off = jnp.where(kt_raw >= 0, 0, 1)
                q_head = functools.partial(lambda off, sl: qw_ref[off, :, sl], off)
            else:
                q_head = functools.partial(lambda x, sl: q_all[x][:, sl], x)
            mask = causal if back == 0 else band if back * TQ == WINDOW else no_mask
            win_work.append(tile_work(kwin_ref, vwin_ref, jnp.maximum(kt_raw, 0), q_head, mask, mw_ref, accw_ref, x))

    flat = lambda tiles: [pair for tile_pairs in tiles for pair in tile_pairs]
    run(flat(win_work[:n_early]))

    lane = lax.broadcasted_iota(jnp.int32, (n_cmp, TQ), 1)
    cmp_end = lax.broadcasted_iota(jnp.int32, (n_cmp, TQ), 0) * CMP_STRIDE + (CMP_BLOCK - 1)
    valid_c = jnp.concatenate([per_head(cmp_end <= tile[x] * TQ + lane) for x in range(2)], axis=1)
    s = jnp.dot(kc_ref[...], jnp.concatenate(q_all, axis=1), preferred_element_type=F32)
    s = jnp.where(valid_c, s, MASK_VALUE)
    e = jnp.exp2(s - jnp.max(s, axis=0, keepdims=True))
    p = jnp.where(valid_c, e * (1.0 / jnp.sum(e, axis=0, keepdims=True)), 0.0)
    o_cmp = jnp.dot(vc_ref[...], p.astype(BF16), preferred_element_type=F32)
    qa_ref[0] = q_all[0]
    p_sum = p[:, N_QL:N_QL + TQ]
    for h in range(1, HEADS_PER_GROUP):
        p_sum = p_sum + p[:, N_QL + h * TQ:N_QL + (h + 1) * TQ]
    p_hi = p_sum.astype(BF16)
    p_lo = (p_sum - p_hi.astype(F32)).astype(BF16)
    ovl = ovl_ref[...]
    imp = (jnp.dot(ovl, p_hi, preferred_element_type=F32)
           + jnp.dot(ovl, p_lo, preferred_element_type=F32))
    j_idx = lax.broadcasted_iota(jnp.int32, (n_sel_blocks, TQ), 0)
    pos_t = tile[1] * TQ + lax.broadcasted_iota(jnp.int32, (n_sel_blocks, TQ), 1)
    cur = pos_t // SEL_BLOCK
    imp = jnp.where(j_idx * SEL_BLOCK > pos_t, -1.0, imp)
    imp = jnp.where(j_idx == 0, FORCE_SCORE, imp)
    imp = jnp.where(j_idx == cur, FORCE_SCORE, imp)
    imp = jnp.where(j_idx == cur - 1, FORCE_SCORE, imp)
    grp = 8
    j_loc = lax.broadcasted_iota(jnp.int32, (grp, TQ), 0)
    bias = []
    for g0 in range(0, n_sel_blocks, grp):
        imp_g = imp[g0:g0 + grp]
        rank = jnp.zeros((grp, TQ), F32)
        for jp in range(n_sel_blocks):
            row = imp[jp:jp + 1, :]
            if jp >= g0 + grp - 1:
                rank = jnp.where(row > imp_g, rank + 1.0, rank)
            elif jp < g0:
                rank = jnp.where(row >= imp_g, rank + 1.0, rank)
            else:
                tie = jnp.where(j_loc > jp - g0, 1.0, 0.0)
                rank = rank + jnp.where(row > imp_g, 1.0, jnp.where(row == imp_g, tie, 0.0))
        bias.append(jnp.where(rank < k_eff, 0.0, SEL_BIAS))
    qa_ref[1] = q_operand(1, bias)

    sel_work = []
    for i in range(n_tiles - 1):
        x = jnp.where(i >= t, 1, 0)
        kt = jnp.where(i >= t, i - t, i)
        sel_work.append(tile_work(ksel_ref, vsel_ref, kt, functools.partial(lambda x, sl: qa_ref[x, :, sl], x),
                                  no_mask, m_ref, acc_ref, x))
    for x in range(2):
        sel_work.append(tile_work(ksel_ref, vsel_ref, tile[x], functools.partial(lambda x, sl: qa_ref[x, :, sl], x),
                                  causal, m_ref, acc_ref, x))

    rest = win_work[n_early:]
    work = []
    for i in range(max(len(sel_work), len(rest))):
        if i < len(rest):
            work += [pair for both in zip(sel_work[i], rest[i]) for pair in both]
        else:
            work += sel_work[i]
    run(work)

    for x in range(2):
        gates_t = gate_ref[rows_of(x), :].T[GATE_LANE0:]
        o_sel = result(acc_ref, x)
        o_win = result(accw_ref, x)
        outs = []
        for h in range(HEADS_PER_GROUP):
            sl = slice(h * TQ, (h + 1) * TQ)
            o = (gates_t[h:h + 1] * o_cmp[:HEAD_DIM, x * N_QL + h * TQ:x * N_QL + (h + 1) * TQ]
                 + gates_t[HEADS_PER_GROUP + h:HEADS_PER_GROUP + h + 1] * o_sel[:HEAD_DIM, sl]
                 + gates_t[2 * HEADS_PER_GROUP + h:2 * HEADS_PER_GROUP + h + 1] * o_win[:HEAD_DIM, sl])
            outs.append(o)
        for hp in range(HEADS_PER_GROUP // 2):
            pair = jnp.concatenate(outs[2 * hp:2 * hp + 2], axis=0).T
            o_ref[rows_of(x), hp * SLAB:(hp + 1) * SLAB] = pair.astype(BF16)


def _attn_kernel(*refs, n_tiles, **static):
    @pl.loop(0, n_tiles // 2)
    def _(t):
        _attn_pair(t, *refs, n_tiles=n_tiles, **static)


def _attention(q, kc, vc, ksel, vsel, kwin, vwin, gates, ovl, batch, seq):
    n = q.shape[0]
    n_cmp = kc.shape[2]
    n_sel_blocks = seq // SEL_BLOCK
    n_tiles = seq // TQ
    width = HEADS_PER_GROUP * HEAD_DIM
    bg = lambda b, g: (b, g)
    return pl.pallas_call(
        functools.partial(_attn_kernel, n_tiles=n_tiles, n_sel_blocks=n_sel_blocks,
                          k_eff=min(N_SEL_BLOCKS, n_sel_blocks), n_early=N_EARLY_WINDOW_TILES),
        grid=(batch, N_GROUPS),
        in_specs=[
            pl.BlockSpec((seq, HEADS_PER_GROUP * SLAB), bg),
            pl.BlockSpec((None, None, n_cmp, SLAB), lambda b, g: (b, g, 0, 0)),
            pl.BlockSpec((None, None, SLAB, n_cmp), lambda b, g: (b, g, 0, 0)),
            pl.BlockSpec((seq, SLAB), bg),
            pl.BlockSpec((None, V_ROWS, seq), lambda b, g: (b, g, 0)),
            pl.BlockSpec((seq, SLAB), bg),
            pl.BlockSpec((None, V_ROWS, seq), lambda b, g: (b, g, 0)),
            pl.BlockSpec((seq, SLAB), bg),
            _const_spec((n_sel_blocks, n_cmp)),
        ],
        out_specs=pl.BlockSpec((seq, width), bg),
        out_shape=jax.ShapeDtypeStruct((n, N_GROUPS * width), BF16),
        scratch_shapes=[pltpu.VMEM((2, SLAB, N_QL), BF16), pltpu.VMEM((2, SLAB, N_QL), BF16),
                        pltpu.VMEM((2, 1, N_QL), F32), pltpu.VMEM((2, V_ROWS, N_QL), F32),
                        pltpu.VMEM((2, 1, N_QL), F32), pltpu.VMEM((2, V_ROWS, N_QL), F32)],
        compiler_params=_params(2),
        name="attention",
    )(q, kc, vc, ksel, vsel, kwin, vwin, gates, ovl)


HALO = 8


def _merge_kernel(x_ref, a_ref, cu_ref, halo_ref, bg_ref, g_ref, wm_ref, wa_ref, wc_ref, wo_ref, cw_ref, o_ref,
                  *, blocks_per_seq):
    for r in range(SUB_TILES):
        rows = pl.ds(r * TM, TM)
        att = jnp.dot(a_ref[rows, :], wa_ref[...], preferred_element_type=F32)

        cu = cu_ref[rows, :]
        if r == 0:
            first = pl.program_id(0) % blocks_per_seq == 0
            halo = jnp.where(first, 0.0, halo_ref[...])
        else:
            halo = cu_ref[pl.ds(r * TM - HALO, HALO), :]
        row = lax.broadcasted_iota(jnp.int32, cu.shape, 0)
        prev1 = jnp.where(row == 0, halo[HALO - 1:HALO, :], pltpu.roll(cu, 1, 0))
        prev2 = jnp.where(row == 0, halo[HALO - 2:HALO - 1, :],
                          jnp.where(row == 1, halo[HALO - 1:HALO, :], pltpu.roll(cu, 2, 0)))
        conv = cw_ref[0:1, :] * prev2 + cw_ref[1:2, :] * prev1 + cw_ref[2:3, :] * cu
        c = (bg_ref[rows, :] * conv).astype(BF16)
        cb = jnp.dot(c, wc_ref[...], preferred_element_type=F32)

        x = x_ref[rows, :]
        h_parts, _ = _norm_parts(x, g_ref[...])
        mg = jax.nn.sigmoid(_dot_parts(h_parts, wm_ref[...]))
        merged =(mg[:, :D_MODEL] * att + mg[:, D_MODEL:] * cb).astype(BF16)
        o_ref[rows, :] = x + jnp.dot(merged, wo_ref[...], preferred_element_type=F32)


def _merge(x1, a, cu, bg, g, wm, wa, wc, wo, cw, seq):
    n = x1.shape[0]
    row = lambda i: (i, 0)
    return pl.pallas_call(
        functools.partial(_merge_kernel, blocks_per_seq=seq // TB),
        grid=(n // TB,),
        in_specs=[
            pl.BlockSpec((TB, D_MODEL), row),
            pl.BlockSpec((TB, ATTN_WIDTH), row),
            pl.BlockSpec((TB, CONV_WIDTH), row),
            pl.BlockSpec((HALO, CONV_WIDTH), lambda i: (jnp.maximum(i * (TB // HALO) - 1, 0), 0)),
            pl.BlockSpec((TB, CONV_WIDTH), row),
            _const_spec((1, D_MODEL)),
            _const_spec((D_MODEL, 2 * D_MODEL)),
            _const_spec((ATTN_WIDTH, D_MODEL)),
            _const_spec((CONV_WIDTH, D_MODEL)),
            _const_spec((D_MODEL, D_MODEL)),
            _const_spec((CONV_KERNEL, CONV_WIDTH)),
        ],
        out_specs=pl.BlockSpec((TB, D_MODEL), row),
        out_shape=jax.ShapeDtypeStruct((n, D_MODEL), F32),
        compiler_params=_params(1),
        name="merge",
    )(x1, a, cu, cu, bg, g, wm, wa, wc, wo, cw)


def _pad_heads(w, n_heads):
    rows = w.shape[0]
    w = w.reshape(rows, n_heads, HEAD_DIM)
    return jnp.pad(w, ((0, 0), (0, 0), (0, SLAB - HEAD_DIM))).reshape(rows, n_heads * SLAB)


def _pad_lanes(v):
    return jnp.pad(v, ((0, 0), (0, SLAB - v.shape[-1])))


def _rope_tables(seq):
    inv_freq = ROPE_THETA ** (-jnp.arange(0, ROPE_DIM, 2, dtype=F32) / ROPE_DIM)
    ang = jnp.arange(seq, dtype=F32)[:, None] * inv_freq[None, :]
    cos, sin = jnp.cos(ang), jnp.sin(ang)
    zeros = jnp.zeros((seq, SLAB - ROPE_DIM), F32)
    half0 = jnp.zeros((seq, ROPE_HALF), F32)
    cos_t = jnp.concatenate([cos, cos, jnp.ones((seq, SLAB - ROPE_DIM), F32)], axis=1)
    sin_lo = jnp.concatenate([-sin, half0, zeros], axis=1)
    sin_hi = jnp.concatenate([half0, sin, zeros], axis=1)
    return cos_t, sin_lo, sin_hi


def _block_onehot(seq):
    blk = jnp.arange(seq, dtype=jnp.int32) // SEL_BLOCK
    return (jnp.arange(SLAB, dtype=jnp.int32)[None, :] == (blk[:, None] + SEL_LANE0)).astype(F32)


def _overlap_t(seq, n_cmp):
    n = np.arange(n_cmp)
    j = np.arange(seq // SEL_BLOCK)
    cs, ce = n * CMP_STRIDE, n * CMP_STRIDE + CMP_BLOCK
    ss, se = j * SEL_BLOCK, j * SEL_BLOCK + SEL_BLOCK
    real = n < (seq // CMP_STRIDE - CMP_BLOCK // CMP_STRIDE + 1)
    ovl = (cs[None, :] < se[:, None]) & (ce[None, :] > ss[:, None]) & real[None, :]
    return jnp.asarray(ovl.astype(np.float32), dtype=BF16)


def _expand_cmp_w1(w1):
    half = CMP_BLOCK // CMP_STRIDE
    w = w1.astype(BF16).reshape(half, CMP_STRIDE, 1, HEAD_DIM, 1, CMP_HIDDEN)
    same_group = jnp.eye(N_GROUPS, dtype=bool).reshape(1, 1, N_GROUPS, 1, N_GROUPS, 1)
    big = jnp.where(same_group, w, jnp.zeros((), BF16))
    return big.reshape(half, N_CHUNKS_ROW, N_GROUPS * CMP_HIDDEN)


def _expand_pe(pe):
    half = CMP_BLOCK // CMP_STRIDE
    p = pe.reshape(half, CMP_STRIDE, 1, HEAD_DIM)
    return jnp.broadcast_to(p, (half, CMP_STRIDE, N_GROUPS, HEAD_DIM)).reshape(half, N_CHUNKS_ROW)


def kernel(x, ffn1_norm_g, ffn1_w_gate, ffn1_w_up, ffn1_w_down, mix_norm_g, w_in, q_norm_g, k_norm_g, cmp_pe_k, cmp_pe_v, cmp_k_w1, cmp_k_w2, cmp_v_w1, cmp_v_w2, conv_w, w_attn_branch, w_conv_branch, w_out, ffn2_norm_g, ffn2_w_gate, ffn2_w_up, ffn2_w_down):
    batch, seq, d_model = x.shape
    depth = ffn1_norm_g.shape[0]
    n_chunks = seq // CMP_STRIDE
    assert d_model == D_MODEL and seq % TB == 0 and seq % (2 * TQ) == 0 and sum(FF_CHUNKS) == D_FF
    assert seq // SEL_BLOCK + SEL_LANE0 <= GATE_LANE0 and GATE_LANE0 + 3 * HEADS_PER_GROUP <= SLAB
    assert n_chunks % SLAB == 0 and WINDOW % TQ == 0 and seq % TBI == 0
    assert seq // 2 <= min(N_SEL_BLOCKS, seq // SEL_BLOCK) * SEL_BLOCK
    cos_t, sin_lo, sin_hi = _rope_tables(seq)
    onehot = _block_onehot(seq)
    ovl = _overlap_t(seq, n_chunks)
    xs = x.reshape(batch * seq, D_MODEL)

    for l in range(depth):
        xs = _ffn(xs, ffn1_norm_g[l][None], ffn1_w_gate[l].astype(BF16), ffn1_w_up[l].astype(BF16),
                  ffn1_w_down[l].astype(BF16))

        w = w_in[l].astype(BF16)
        c_kv = ATTN_WIDTH
        c_gate = c_kv + N_KV_SLOTS * KV_WIDTH
        c_conv = c_gate + N_GATES
        c_merge = c_conv + 3 * CONV_WIDTH
        kv = [w[:, c_kv + s * KV_WIDTH:c_kv + (s + 1) * KV_WIDTH] for s in range(N_KV_SLOTS)]
        wgate = w[:, c_gate:c_conv].reshape(D_MODEL, 3, N_GROUPS, HEADS_PER_GROUP).transpose(0, 2, 1, 3)
        wksel = jnp.concatenate([
            kv[2].reshape(D_MODEL, N_GROUPS, HEAD_DIM),
            jnp.zeros((D_MODEL, N_GROUPS, GATE_LANE0 - HEAD_DIM), BF16),
            wgate.reshape(D_MODEL, N_GROUPS, 3 * HEADS_PER_GROUP),
            jnp.zeros((D_MODEL, N_GROUPS, SLAB - GATE_LANE0 - 3 * HEADS_PER_GROUP), BF16)],
            axis=2).reshape(D_MODEL, KV_PAD)
        w_b = jnp.concatenate([
            _pad_heads(w[:, :ATTN_WIDTH], N_HEADS), wksel, _pad_heads(kv[4], N_GROUPS), kv[3], kv[5],
            kv[0], kv[1], w[:, c_conv:c_merge]], axis=1)
        q, ksel, kwin, vsel, vwin, kcmp, vcmp, gates, cu, bg = _inproj(
            xs, mix_norm_g[l][None], w_b, _pad_lanes(q_norm_g[l][None]), _pad_lanes(k_norm_g[l][1:3]),
            cos_t, sin_lo, sin_hi, onehot, seq)

        w2k = _pad_lanes(cmp_k_w2[l]).astype(BF16)
        w2v = _pad_lanes(cmp_v_w2[l]).astype(BF16)
        kc, vc = _compress(kcmp, vcmp, _expand_pe(cmp_pe_k[l]), _expand_pe(cmp_pe_v[l]),
                           _expand_cmp_w1(cmp_k_w1[l]), _expand_cmp_w1(cmp_v_w1[l]), w2k, w2v,
                           _pad_lanes(k_norm_g[l][0:1]))

        a = _attention(q, kc, vc, ksel, vsel, kwin, vwin, gates, ovl, batch, seq)

        xs = _merge(xs, a, cu, bg, mix_norm_g[l][None], w[:, c_merge:], w_attn_branch[l].astype(BF16),
                    w_conv_branch[l].astype(BF16), w_out[l].astype(BF16), conv_w[l], seq)

        xs = _ffn(xs, ffn2_norm_g[l][None], ffn2_w_gate[l].astype(BF16), ffn2_w_up[l].astype(BF16),
                  ffn2_w_down[l].astype(BF16))
    return xs.reshape(batch, seq, D_MODEL)
```

```python
import functools
import math

import numpy as np
import jax
import jax.numpy as jnp
from jax import lax
from jax.experimental import pallas as pl
from jax.experimental.pallas import tpu as pltpu

D_MODEL = 1024
N_HEADS = 8
N_GROUPS = 2
HEADS_PER_GROUP = N_HEADS // N_GROUPS
HEAD_DIM = 64
ATTN_WIDTH = N_HEADS * HEAD_DIM
KV_WIDTH = N_GROUPS * HEAD_DIM
N_KV_SLOTS = 6
ROPE_DIM = HEAD_DIM // 4
ROPE_HALF = ROPE_DIM // 2
ROPE_THETA = 500000.0
CMP_BLOCK = 32
CMP_STRIDE = 16
CMP_HIDDEN = 4 * HEAD_DIM
SEL_BLOCK = 64
N_SEL_BLOCKS = 16
WINDOW = 512
CONV_WIDTH = 512
CONV_KERNEL = 3
D_FF = 2816
NORM_EPS = 1e-6
MASK_VALUE = -1e30
FORCE_SCORE = 1e4
N_GATES = 3 * N_HEADS

LANES = 128
SLAB = LANES
MXU_DIM = 256
Q_PAD = N_HEADS * SLAB
KV_PAD = N_GROUPS * SLAB
V_ROWS = 80
VT_PAD = N_GROUPS * V_ROWS
SEL_BIAS = -float(2 ** 30)
SEL_LANE0 = HEAD_DIM
GATE_LANE0 = SLAB - 32
Q_SCALE = HEAD_DIM ** -0.5 * math.log2(math.e)

TM = 512
SUB_TILES = 2
TB = TM * SUB_TILES
INPROJ_TILES = 1
TBI = TM * INPROJ_TILES
NORM_PARTS = 2
TQ = 256
FF_CHUNKS = (6 * MXU_DIM, 5 * MXU_DIM)
VMEM_LIMIT = 56 * 1024 * 1024

F32 = jnp.float32
BF16 = jnp.bfloat16


def _const_spec(shape):
    return pl.BlockSpec(shape, lambda *_: (0,) * len(shape), pipeline_mode=pl.Buffered(1))


def _params(n_axes):
    return pltpu.CompilerParams(dimension_semantics=("parallel",) * n_axes, vmem_limit_bytes=VMEM_LIMIT)


def _rms_rows(x, g):
    ms = jnp.mean(x * x, axis=-1, keepdims=True)
    return x * lax.rsqrt(ms + NORM_EPS) * g


def _norm_parts(x, g):
    part = x.shape[0] // NORM_PARTS
    parts = [_rms_rows(x[r:r + part], g).astype(BF16) for r in range(0, x.shape[0], part)]
    return parts, jnp.concatenate(parts, axis=0)


def _dot_parts(parts, w):
    return jnp.concatenate([jnp.dot(p, w, preferred_element_type=F32) for p in parts], axis=0)


def _ffn_kernel(x_ref, g_ref, wg_ref, wu_ref, wd_ref, o_ref):
    for r in range(SUB_TILES):
        rows = slice(r * TM, (r + 1) * TM)
        x = x_ref[rows, :]
        h_parts, h = _norm_parts(x, g_ref[...])
        acc = jnp.zeros(x.shape, F32)
        c0 = 0
        for width in FF_CHUNKS:
            sl = slice(c0, c0 + width)
            if c0 == 0:
                gate = _dot_parts(h_parts, wg_ref[:, sl])
            else:
                gate = jnp.dot(h, wg_ref[:, sl], preferred_element_type=F32)
            up = jnp.dot(h, wu_ref[:, sl], preferred_element_type=F32)
            act = (gate * jax.nn.sigmoid(gate) * up).astype(BF16)
            acc = acc + jnp.dot(act, wd_ref[sl, :], preferred_element_type=F32)
            c0 += width
        o_ref[rows, :] = x + 0.5 * acc


def _ffn(x, g, wg, wu, wd):
    n = x.shape[0]
    return pl.pallas_call(
        _ffn_kernel,
        grid=(n // TB,),
        in_specs=[
            pl.BlockSpec((TB, D_MODEL), lambda i: (i, 0)),
            _const_spec((1, D_MODEL)),
            _const_spec((D_MODEL, D_FF)),
            _const_spec((D_MODEL, D_FF)),
            _const_spec((D_FF, D_MODEL)),
        ],
        out_specs=pl.BlockSpec((TB, D_MODEL), lambda i: (i, 0)),
        out_shape=jax.ShapeDtypeStruct((n, D_MODEL), F32),
        compiler_params=_params(1),
        name="ffn",
    )(x, g, wg, wu, wd)


_C_Q = 0
_C_KSEL = _C_Q + Q_PAD
_C_KWIN = _C_KSEL + KV_PAD
_C_V = _C_KWIN + KV_PAD
_C_KCMP = _C_V + 2 * KV_WIDTH
_C_VCMP = _C_KCMP + KV_WIDTH
_C_CONV = _C_VCMP + KV_WIDTH
_C_END = _C_CONV + 3 * CONV_WIDTH
N_CHUNKS_ROW = CMP_STRIDE * KV_WIDTH


def _norm_rope_slab(x, g, cos, sin_lo, sin_hi):
    ms = jnp.sum(x * x, axis=-1, keepdims=True) * (1.0 / HEAD_DIM)
    y = x * lax.rsqrt(ms + NORM_EPS) * g
    return y * cos + pltpu.roll(y, ROPE_HALF, 1) * sin_hi + pltpu.roll(y, SLAB - ROPE_HALF, 1) * sin_lo


def _inproj_kernel(x_ref, g_ref, w_ref, qg_ref, kg_ref, cos_ref, slo_ref, shi_ref, oh_ref,
                   q_ref, ksel_ref, kwin_ref, vsel_ref, vwin_ref, kcmp_ref, vcmp_ref, gate_ref, cu_ref, bg_ref,
                   stage_ref):
    for r in range(INPROJ_TILES):
        rows = pl.ds(r * TM, TM)
        chunk_rows = pl.ds(r * (TM // CMP_STRIDE), TM // CMP_STRIDE)
        _inproj_tile(x_ref.at[rows], g_ref, w_ref, qg_ref, kg_ref,
                     cos_ref.at[rows], slo_ref.at[rows], shi_ref.at[rows], oh_ref.at[rows],
                     q_ref.at[rows], ksel_ref.at[rows], kwin_ref.at[rows], vsel_ref.at[:, rows], vwin_ref.at[:, rows],
                     kcmp_ref.at[chunk_rows], vcmp_ref.at[chunk_rows], gate_ref.at[rows], cu_ref.at[rows],
                     bg_ref.at[rows], stage_ref.at[r])


def _inproj_tile(x_ref, g_ref, w_ref, qg_ref, kg_ref, cos_ref, slo_ref, shi_ref, oh_ref,
                 q_ref, ksel_ref, kwin_ref, vsel_ref, vwin_ref, kcmp_ref, vcmp_ref, gate_ref, cu_ref, bg_ref,
                 stage_ref):
    h_parts, h = _norm_parts(x_ref[...], g_ref[...])
    cos, slo, shi = cos_ref[...], slo_ref[...], shi_ref[...]

    def proj(c0, width):
        return jnp.dot(h, w_ref[:, c0:c0 + width], preferred_element_type=F32)

    def emit_v():
        v_t = proj(_C_V, 2 * KV_WIDTH).T
        tail = jnp.where(lax.broadcasted_iota(jnp.int32, (V_ROWS - HEAD_DIM, TM), 0) == 0, 1.0, 0.0)
        for b, v_ref in enumerate((vsel_ref, vwin_ref)):
            for g in range(N_GROUPS):
                r0 = (b * N_GROUPS + g) * HEAD_DIM
                v_ref[g * V_ROWS:(g + 1) * V_ROWS, :] = jnp.concatenate([v_t[r0:r0 + HEAD_DIM], tail],
                                                                        axis=0).astype(BF16)

    def emit_q():
        q = _dot_parts(h_parts, w_ref[:, _C_Q:_C_Q + Q_PAD])
        qg = qg_ref[...]
        for hd in range(N_HEADS):
            sl = slice(hd * SLAB, (hd + 1) * SLAB)
            y = _norm_rope_slab(q[:, sl], qg, cos, slo, shi) * Q_SCALE
            q_ref[:, sl] = y.astype(BF16)

    def emit_k():
        ksel = proj(_C_KSEL, KV_PAD)
        kwin = proj(_C_KWIN, KV_PAD)
        oh = oh_ref[...]
        gate_ref[...] = jax.nn.sigmoid(ksel)
        head_lanes = lax.broadcasted_iota(jnp.int32, (1, SLAB), 1) < HEAD_DIM
        for g in range(N_GROUPS):
            sl = slice(g * SLAB, (g + 1) * SLAB)
            k_g = jnp.where(head_lanes, ksel[:, sl], 0.0)
            ksel_ref[:, sl] = (_norm_rope_slab(k_g, kg_ref[0:1, :], cos, slo, shi) + oh).astype(BF16)
            kwin_ref[:, sl] = _norm_rope_slab(kwin[:, sl], kg_ref[1:2, :], cos, slo, shi).astype(BF16)

    def emit_cmp():
        kv_cmp = proj(_C_KCMP, 2 * KV_WIDTH)
        for c, out_ref in enumerate((kcmp_ref, vcmp_ref)):
            stage_ref[c] = kv_cmp[:, c * KV_WIDTH:(c + 1) * KV_WIDTH]
            for tok in range(CMP_STRIDE):
                out_ref[:, tok * KV_WIDTH:(tok + 1) * KV_WIDTH] = stage_ref[c, pl.ds(tok, TM // CMP_STRIDE,
                                                                                     stride=CMP_STRIDE), :]

    def emit_conv():
        conv = proj(_C_CONV, 3 * CONV_WIDTH)
        bg_ref[...] = conv[:, :CONV_WIDTH]
        cu_ref[...] = conv[:, CONV_WIDTH:2 * CONV_WIDTH] * conv[:, 2 * CONV_WIDTH:]

    for emit in (emit_q, emit_k, emit_v, emit_cmp, emit_conv):
        emit()


def _inproj(x1, g, w_b, qg, kg, cos, slo, shi, oh, seq):
    n = x1.shape[0]
    tiles_per_seq = seq // TBI
    row = lambda i: (i, 0)
    pos = lambda i: (i % tiles_per_seq, 0)
    vt = (pl.BlockSpec((None, VT_PAD, TBI), lambda i: (i // tiles_per_seq, 0, i % tiles_per_seq)),
          jax.ShapeDtypeStruct((n // seq, VT_PAD, seq), BF16))
    chunks = (pl.BlockSpec((None, TBI // CMP_STRIDE, N_CHUNKS_ROW),
                           lambda i: (i // tiles_per_seq, i % tiles_per_seq, 0)),
              jax.ShapeDtypeStruct((n // seq, seq // CMP_STRIDE, N_CHUNKS_ROW), F32))
    rows = lambda width, dtype: (pl.BlockSpec((TBI, width), row), jax.ShapeDtypeStruct((n, width), dtype))
    outs = [rows(Q_PAD, BF16), rows(KV_PAD, BF16), rows(KV_PAD, BF16), vt, vt, chunks, chunks,
            rows(KV_PAD, F32), rows(CONV_WIDTH, F32), rows(CONV_WIDTH, F32)]
    return pl.pallas_call(
        _inproj_kernel,
        grid=(n // TBI,),
        in_specs=[
            pl.BlockSpec((TBI, D_MODEL), row),
            _const_spec((1, D_MODEL)),
            _const_spec((D_MODEL, _C_END)),
            _const_spec((1, SLAB)),
            _const_spec((2, SLAB)),
            pl.BlockSpec((TBI, SLAB), pos),
            pl.BlockSpec((TBI, SLAB), pos),
            pl.BlockSpec((TBI, SLAB), pos),
            pl.BlockSpec((TBI, SLAB), pos),
        ],
        out_specs=[spec for spec, _ in outs],
        out_shape=[shape for _, shape in outs],
        scratch_shapes=[pltpu.VMEM((INPROJ_TILES, 2, TM, KV_WIDTH), F32)],
        compiler_params=_params(1),
        name="inproj",
    )(x1, g, w_b, qg, kg, cos, slo, shi, oh)


def _compress_kernel(kc_ref, vc_ref, pek_ref, pev_ref, wk1_ref, wv1_ref, wk2_ref, wv2_ref, kg_ref,
                     ko_ref, vo_ref):
    def branch(r_ref, pe_ref, w1_ref, w2_ref):
        r = r_ref[0]
        top = jnp.dot((r + pe_ref[0:1, :]).astype(BF16), w1_ref[0], preferred_element_type=F32)
        bot = jnp.dot((r + pe_ref[1:2, :]).astype(BF16), w1_ref[1], preferred_element_type=F32)
        hid = top + pltpu.roll(bot, bot.shape[0] - 1, 0)
        hid = (hid * jax.nn.sigmoid(hid)).astype(BF16)
        return [jnp.dot(hid[:, g * CMP_HIDDEN:(g + 1) * CMP_HIDDEN], w2_ref[...], preferred_element_type=F32)
                for g in range(N_GROUPS)]

    k = branch(kc_ref, pek_ref, wk1_ref, wk2_ref)
    v = branch(vc_ref, pev_ref, wv1_ref, wv2_ref)
    for g in range(N_GROUPS):
        ms = jnp.sum(k[g] * k[g], axis=-1, keepdims=True) * (1.0 / HEAD_DIM)
        ko_ref[0, g] = (k[g] * lax.rsqrt(ms + NORM_EPS) * kg_ref[...]).astype(BF16)
        vo_ref[0, g] = v[g].T.astype(BF16)


def _compress(kc, vc, pek, pev, wk1, wv1, wk2, wv2, kg):
    b, n_chunks, _ = kc.shape
    blk = pl.BlockSpec((1, n_chunks, N_CHUNKS_ROW), lambda i: (i, 0, 0))
    k_out = pl.BlockSpec((1, N_GROUPS, n_chunks, SLAB), lambda i: (i, 0, 0, 0))
    v_out = pl.BlockSpec((1, N_GROUPS, SLAB, n_chunks), lambda i: (i, 0, 0, 0))
    return pl.pallas_call(
        _compress_kernel,
        grid=(b,),
        in_specs=[blk, blk,
                  _const_spec((2, N_CHUNKS_ROW)), _const_spec((2, N_CHUNKS_ROW)),
                  _const_spec((2, N_CHUNKS_ROW, N_GROUPS * CMP_HIDDEN)),
                  _const_spec((2, N_CHUNKS_ROW, N_GROUPS * CMP_HIDDEN)),
                  _const_spec((CMP_HIDDEN, SLAB)), _const_spec((CMP_HIDDEN, SLAB)),
                  _const_spec((1, SLAB))],
        out_specs=[k_out, v_out],
        out_shape=[jax.ShapeDtypeStruct((b, N_GROUPS, n_chunks, SLAB), BF16),
                   jax.ShapeDtypeStruct((b, N_GROUPS, SLAB, n_chunks), BF16)],
        compiler_params=_params(1),
        name="compress",
    )(kc, vc, pek, pev, wk1, wv1, wk2, wv2, kg)


N_QL = HEADS_PER_GROUP * TQ
N_EARLY_WINDOW_TILES = 3
N_ATTN_SCRATCH = 5
SCORES_AHEAD = 6

def _attn_pair(t, q_ref, kc_ref, vc_ref, ksel_ref, vsel_ref, kwin_ref, vwin_ref, gate_ref, ovl_ref, o_ref,
               qa_ref, m_ref, acc_ref, mw_ref, accw_ref, *, n_tiles, n_sel_blocks, k_eff, n_early):
    tile = (t, n_tiles - 1 - t)
    n_cmp = kc_ref.shape[0]

    def per_head(x):
        return jnp.concatenate([x] * HEADS_PER_GROUP, axis=1)

    def rows_of(x):
        return pl.ds(tile[x] * TQ, TQ)

    def keys(kt):
        return pl.ds(kt * TQ, TQ)

    q_t = [[q_ref[rows_of(x), h * SLAB:(h + 1) * SLAB].astype(F32).T[:HEAD_DIM] for h in range(HEADS_PER_GROUP)]
           for x in range(2)]

    def q_operand(x, extra_rows):
        n_pad = SLAB - HEAD_DIM - sum(r.shape[0] for r in extra_rows)
        tail = extra_rows + [jnp.zeros((n_pad, TQ), F32)]
        return jnp.concatenate([jnp.concatenate([q_t[x][h]] + tail, axis=0) for h in range(HEADS_PER_GROUP)],
                               axis=1).astype(BF16)

    q_all = [q_operand(x, []) for x in range(2)]

    for mr, ar in ((m_ref, acc_ref), (mw_ref, accw_ref)):
        mr[...] = jnp.full(mr.shape, MASK_VALUE, F32)
        ar[...] = jnp.zeros(ar.shape, F32)

    key_r = lax.broadcasted_iota(jnp.int32, (TQ, TQ), 0)
    qry_c = lax.broadcasted_iota(jnp.int32, (TQ, TQ), 1)

    def head_step(mr, ar, x, sl, s, v_t):
        m_old = mr[x, :, sl]
        m_new = jnp.maximum(m_old, jnp.max(s, axis=0, keepdims=True))
        p = jnp.exp2(s - m_new).astype(BF16)
        ar[x, :, sl] = jnp.exp2(m_old - m_new) * ar[x, :, sl] + jnp.dot(v_t, p, preferred_element_type=F32)
        mr[x, :, sl] = m_new

    def tile_work(k_ref, v_ref, kt, q_head, mask, mr, ar, x):
        work = []
        for h in range(HEADS_PER_GROUP):
            sl = slice(h * TQ, (h + 1) * TQ)
            work.append((
                functools.partial(lambda sl: mask(jnp.dot(k_ref[keys(kt), :], q_head(sl),
                                                          preferred_element_type=F32)), sl),
                functools.partial(lambda sl, s: head_step(mr, ar, x, sl, s, v_ref[:, keys(kt)]), sl)))
        return work

    def result(ar, x):
        acc = ar[x]
        return acc * (1.0 / acc[HEAD_DIM:HEAD_DIM + 1, :])

    def run(work, ahead=SCORES_AHEAD):
        pending = [w[0]() for w in work[:ahead]]
        for i, (_, update) in enumerate(work):
            if i + ahead < len(work):
                pending.append(work[i + ahead][0]())
            update(pending.pop(0))

    diff0 = qry_c - key_r
    no_mask = lambda s: s
    causal = lambda s: jnp.where(diff0 >= 0, s, MASK_VALUE)
    band = lambda s: jnp.where(diff0 < 0, s, MASK_VALUE)

    win_work = []
    for x in range(2):
        for back in range(WINDOW // TQ, -1, -1):
            kt = tile[x] - back
            if kt < 0:
                continue
            q_head = functools.partial(lambda x, sl: q_all[x][:, sl], x)
            mask = causal if back == 0 else band if back * TQ == WINDOW else no_mask
            win_work.append(tile_work(kwin_ref, vwin_ref, kt, q_head, mask, mw_ref, accw_ref, x))

    flat = lambda tiles: [pair for tile_pairs in tiles for pair in tile_pairs]
    run(flat(win_work[:n_early]))

    lane = lax.broadcasted_iota(jnp.int32, (n_cmp, TQ), 1)
    cmp_end = lax.broadcasted_iota(jnp.int32, (n_cmp, TQ), 0) * CMP_STRIDE + (CMP_BLOCK - 1)
    valid_c = jnp.concatenate([per_head(cmp_end <= tile[x] * TQ + lane) for x in range(2)], axis=1)
    s = jnp.dot(kc_ref[...], jnp.concatenate(q_all, axis=1), preferred_element_type=F32)
    s = jnp.where(valid_c, s, MASK_VALUE)
    e = jnp.exp2(s - jnp.max(s, axis=0, keepdims=True))
    p = jnp.where(valid_c, e * (1.0 / jnp.sum(e, axis=0, keepdims=True)), 0.0)
    o_cmp = jnp.dot(vc_ref[...], p.astype(BF16), preferred_element_type=F32)
    qa_ref[0] = q_all[0]
    p_sum = p[:, N_QL:N_QL + TQ]
    for h in range(1, HEADS_PER_GROUP):
        p_sum = p_sum + p[:, N_QL + h * TQ:N_QL + (h + 1) * TQ]
    p_hi = p_sum.astype(BF16)
    p_lo = (p_sum - p_hi.astype(F32)).astype(BF16)
    ovl = ovl_ref[...]
    imp = (jnp.dot(ovl, p_hi, preferred_element_type=F32)
           + jnp.dot(ovl, p_lo, preferred_element_type=F32))
    j_idx = lax.broadcasted_iota(jnp.int32, (n_sel_blocks, TQ), 0)
    pos_t = tile[1] * TQ + lax.broadcasted_iota(jnp.int32, (n_sel_blocks, TQ), 1)
    cur = pos_t // SEL_BLOCK
    imp = jnp.where(j_idx * SEL_BLOCK > pos_t, -1.0, imp)
    imp = jnp.where(j_idx == 0, FORCE_SCORE, imp)
    imp = jnp.where(j_idx == cur, FORCE_SCORE, imp)
    imp = jnp.where(j_idx == cur - 1, FORCE_SCORE, imp)
    grp = 8
    j_loc = lax.broadcasted_iota(jnp.int32, (grp, TQ), 0)
    bias = []
    for g0 in range(0, n_sel_blocks, grp):
        imp_g = imp[g0:g0 + grp]
        rank = jnp.zeros((grp, TQ), F32)
        for jp in range(n_sel_blocks):
            row = imp[jp:jp + 1, :]
            if jp >= g0 + grp - 1:
                rank = jnp.where(row > imp_g, rank + 1.0, rank)
            elif jp < g0:
                rank = jnp.where(row >= imp_g, rank + 1.0, rank)
            else:
                tie = jnp.where(j_loc > jp - g0, 1.0, 0.0)
                rank = rank + jnp.where(row > imp_g, 1.0, jnp.where(row == imp_g, tie, 0.0))
        bias.append(jnp.where(rank < k_eff, 0.0, SEL_BIAS))
    qa_ref[1] = q_operand(1, bias)

    sel_work = []
    for i in range(n_tiles - 1):
        x, kt = (1, i - t) if i >= t else (0, i)
        sel_work.append(tile_work(ksel_ref, vsel_ref, kt, functools.partial(lambda x, sl: qa_ref[x, :, sl], x),
                                  no_mask, m_ref, acc_ref, x))
    for x in range(2):
        sel_work.append(tile_work(ksel_ref, vsel_ref, tile[x], functools.partial(lambda x, sl: qa_ref[x, :, sl], x),
                                  causal, m_ref, acc_ref, x))

    rest = win_work[n_early:]
    work = []
    for i in range(max(len(sel_work), len(rest))):
        if i < len(rest):
            work += [pair for both in zip(sel_work[i], rest[i]) for pair in both]
        else:
            work += sel_work[i]
    run(work)

    for x in range(2):
        gates_t = gate_ref[rows_of(x), :].T[GATE_LANE0:]
        o_sel = result(acc_ref, x)
        o_win = result(accw_ref, x)
        outs = []
        for h in range(HEADS_PER_GROUP):
            sl = slice(h * TQ, (h + 1) * TQ)
            o = (gates_t[h:h + 1] * o_cmp[:HEAD_DIM, x * N_QL + h * TQ:x * N_QL + (h + 1) * TQ]
                 + gates_t[HEADS_PER_GROUP + h:HEADS_PER_GROUP + h + 1] * o_sel[:HEAD_DIM, sl]
                 + gates_t[2 * HEADS_PER_GROUP + h:2 * HEADS_PER_GROUP + h + 1] * o_win[:HEAD_DIM, sl])
            outs.append(o)
        for hp in range(HEADS_PER_GROUP // 2):
            pair = jnp.concatenate(outs[2 * hp:2 * hp + 2], axis=0).T
            o_ref[rows_of(x), hp * SLAB:(hp + 1) * SLAB] = pair.astype(BF16)


def _attn_kernel(*refs, n_tiles, **static):
    n_io = len(refs) - N_ATTN_SCRATCH
    io_refs, scratch = refs[:n_io], refs[n_io:]
    for t in range(n_tiles // 2):
        _attn_pair(t, *io_refs, *[ref.at[t] for ref in scratch], n_tiles=n_tiles, **static)


def _attention(q, kc, vc, ksel, vsel, kwin, vwin, gates, ovl, batch, seq):
    n = q.shape[0]
    n_cmp = kc.shape[2]
    n_sel_blocks = seq // SEL_BLOCK
    n_tiles = seq // TQ
    width = HEADS_PER_GROUP * HEAD_DIM
    bg = lambda b, g: (b, g)
    return pl.pallas_call(
        functools.partial(_attn_kernel, n_tiles=n_tiles, n_sel_blocks=n_sel_blocks,
                          k_eff=min(N_SEL_BLOCKS, n_sel_blocks), n_early=N_EARLY_WINDOW_TILES),
        grid=(batch, N_GROUPS),
        in_specs=[
            pl.BlockSpec((seq, HEADS_PER_GROUP * SLAB), bg),
            pl.BlockSpec((None, None, n_cmp, SLAB), lambda b, g: (b, g, 0, 0)),
            pl.BlockSpec((None, None, SLAB, n_cmp), lambda b, g: (b, g, 0, 0)),
            pl.BlockSpec((seq, SLAB), bg),
            pl.BlockSpec((None, V_ROWS, seq), lambda b, g: (b, g, 0)),
            pl.BlockSpec((seq, SLAB), bg),
            pl.BlockSpec((None, V_ROWS, seq), lambda b, g: (b, g, 0)),
            pl.BlockSpec((seq, SLAB), bg),
            _const_spec((n_sel_blocks, n_cmp)),
        ],
        out_specs=pl.BlockSpec((seq, width), bg),
        out_shape=jax.ShapeDtypeStruct((n, N_GROUPS * width), BF16),
        scratch_shapes=[pltpu.VMEM((n_tiles // 2, 2) + shape, dtype) for shape, dtype in (
            ((SLAB, N_QL), BF16), ((1, N_QL), F32), ((V_ROWS, N_QL), F32), ((1, N_QL), F32), ((V_ROWS, N_QL), F32))],
        compiler_params=_params(2),
        name="attention",
    )(q, kc, vc, ksel, vsel, kwin, vwin, gates, ovl)


HALO = 8


def _merge_kernel(x_ref, a_ref, cu_ref, halo_ref, bg_ref, g_ref, wm_ref, wa_ref, wc_ref, wo_ref, cw_ref, o_ref,
                  *, blocks_per_seq):
    for r in range(SUB_TILES):
        rows = pl.ds(r * TM, TM)
        att = jnp.dot(a_ref[rows, :], wa_ref[...], preferred_element_type=F32)

        cu = cu_ref[rows, :]
        if r == 0:
            first = pl.program_id(0) % blocks_per_seq == 0
            halo = jnp.where(first, 0.0, halo_ref[...])
        else:
            halo = cu_ref[pl.ds(r * TM - HALO, HALO), :]
        row = lax.broadcasted_iota(jnp.int32, cu.shape, 0)
        prev1 = jnp.where(row == 0, halo[HALO - 1:HALO, :], pltpu.roll(cu, 1, 0))
        prev2 = jnp.where(row == 0, halo[HALO - 2:HALO - 1, :],
                          jnp.where(row == 1, halo[HALO - 1:HALO, :], pltpu.roll(cu, 2, 0)))
        conv = cw_ref[0:1, :] * prev2 + cw_ref[1:2, :] * prev1 + cw_ref[2:3, :] * cu
        c = (bg_ref[rows, :] * conv).astype(BF16)
        cb = jnp.dot(c, wc_ref[...], preferred_element_type=F32)

        x = x_ref[rows, :]
        h_parts, _ = _norm_parts(x, g_ref[...])
        mg = jax.nn.sigmoid(_dot_parts(h_parts, wm_ref[...]))
        merged =(mg[:, :D_MODEL] * att + mg[:, D_MODEL:] * cb).astype(BF16)
        o_ref[rows, :] = x + jnp.dot(merged, wo_ref[...], preferred_element_type=F32)


def _merge(x1, a, cu, bg, g, wm, wa, wc, wo, cw, seq):
    n = x1.shape[0]
    row = lambda i: (i, 0)
    return pl.pallas_call(
        functools.partial(_merge_kernel, blocks_per_seq=seq // TB),
        grid=(n // TB,),
        in_specs=[
            pl.BlockSpec((TB, D_MODEL), row),
            pl.BlockSpec((TB, ATTN_WIDTH), row),
            pl.BlockSpec((TB, CONV_WIDTH), row),
            pl.BlockSpec((HALO, CONV_WIDTH), lambda i: (jnp.maximum(i * (TB // HALO) - 1, 0), 0)),
            pl.BlockSpec((TB, CONV_WIDTH), row),
            _const_spec((1, D_MODEL)),
            _const_spec((D_MODEL, 2 * D_MODEL)),
            _const_spec((ATTN_WIDTH, D_MODEL)),
            _const_spec((CONV_WIDTH, D_MODEL)),
            _const_spec((D_MODEL, D_MODEL)),
            _const_spec((CONV_KERNEL, CONV_WIDTH)),
        ],
        out_specs=pl.BlockSpec((TB, D_MODEL), row),
        out_shape=jax.ShapeDtypeStruct((n, D_MODEL), F32),
        compiler_params=_params(1),
        name="merge",
    )(x1, a, cu, cu, bg, g, wm, wa, wc, wo, cw)


def _pad_heads(w, n_heads):
    rows = w.shape[0]
    w = w.reshape(rows, n_heads, HEAD_DIM)
    return jnp.pad(w, ((0, 0), (0, 0), (0, SLAB - HEAD_DIM))).reshape(rows, n_heads * SLAB)


def _pad_lanes(v):
    return jnp.pad(v, ((0, 0), (0, SLAB - v.shape[-1])))


def _rope_tables(seq):
    inv_freq = ROPE_THETA ** (-jnp.arange(0, ROPE_DIM, 2, dtype=F32) / ROPE_DIM)
    ang = jnp.arange(seq, dtype=F32)[:, None] * inv_freq[None, :]
    cos, sin = jnp.cos(ang), jnp.sin(ang)
    zeros = jnp.zeros((seq, SLAB - ROPE_DIM), F32)
    half0 = jnp.zeros((seq, ROPE_HALF), F32)
    cos_t = jnp.concatenate([cos, cos, jnp.ones((seq, SLAB - ROPE_DIM), F32)], axis=1)
    sin_lo = jnp.concatenate([-sin, half0, zeros], axis=1)
    sin_hi = jnp.concatenate([half0, sin, zeros], axis=1)
    return cos_t, sin_lo, sin_hi


def _block_onehot(seq):
    blk = jnp.arange(seq, dtype=jnp.int32) // SEL_BLOCK
    return (jnp.arange(SLAB, dtype=jnp.int32)[None, :] == (blk[:, None] + SEL_LANE0)).astype(F32)


def _overlap_t(seq, n_cmp):
    n = np.arange(n_cmp)
    j = np.arange(seq // SEL_BLOCK)
    cs, ce = n * CMP_STRIDE, n * CMP_STRIDE + CMP_BLOCK
    ss, se = j * SEL_BLOCK, j * SEL_BLOCK + SEL_BLOCK
    real = n < (seq // CMP_STRIDE - CMP_BLOCK // CMP_STRIDE + 1)
    ovl = (cs[None, :] < se[:, None]) & (ce[None, :] > ss[:, None]) & real[None, :]
    return jnp.asarray(ovl.astype(np.float32), dtype=BF16)


def _expand_cmp_w1(w1):
    half = CMP_BLOCK // CMP_STRIDE
    w = w1.astype(BF16).reshape(half, CMP_STRIDE, 1, HEAD_DIM, 1, CMP_HIDDEN)
    same_group = jnp.eye(N_GROUPS, dtype=bool).reshape(1, 1, N_GROUPS, 1, N_GROUPS, 1)
    big = jnp.where(same_group, w, jnp.zeros((), BF16))
    return big.reshape(half, N_CHUNKS_ROW, N_GROUPS * CMP_HIDDEN)


def _expand_pe(pe):
    half = CMP_BLOCK // CMP_STRIDE
    p = pe.reshape(half, CMP_STRIDE, 1, HEAD_DIM)
    return jnp.broadcast_to(p, (half, CMP_STRIDE, N_GROUPS, HEAD_DIM)).reshape(half, N_CHUNKS_ROW)


def kernel(x, ffn1_norm_g, ffn1_w_gate, ffn1_w_up, ffn1_w_down, mix_norm_g, w_in, q_norm_g, k_norm_g, cmp_pe_k, cmp_pe_v, cmp_k_w1, cmp_k_w2, cmp_v_w1, cmp_v_w2, conv_w, w_attn_branch, w_conv_branch, w_out, ffn2_norm_g, ffn2_w_gate, ffn2_w_up, ffn2_w_down):
    batch, seq, d_model = x.shape
    depth = ffn1_norm_g.shape[0]
    n_chunks = seq // CMP_STRIDE
    assert d_model == D_MODEL and seq % TB == 0 and seq % (2 * TQ) == 0 and sum(FF_CHUNKS) == D_FF
    assert seq // SEL_BLOCK + SEL_LANE0 <= GATE_LANE0 and GATE_LANE0 + 3 * HEADS_PER_GROUP <= SLAB
    assert n_chunks % SLAB == 0 and WINDOW % TQ == 0 and seq % TBI == 0
    assert seq // 2 <= min(N_SEL_BLOCKS, seq // SEL_BLOCK) * SEL_BLOCK
    cos_t, sin_lo, sin_hi = _rope_tables(seq)
    onehot = _block_onehot(seq)
    ovl = _overlap_t(seq, n_chunks)
    xs = x.reshape(batch * seq, D_MODEL)

    for l in range(depth):
        xs = _ffn(xs, ffn1_norm_g[l][None], ffn1_w_gate[l].astype(BF16), ffn1_w_up[l].astype(BF16),
                  ffn1_w_down[l].astype(BF16))

        w = w_in[l].astype(BF16)
        c_kv = ATTN_WIDTH
        c_gate = c_kv + N_KV_SLOTS * KV_WIDTH
        c_conv = c_gate + N_GATES
        c_merge = c_conv + 3 * CONV_WIDTH
        kv = [w[:, c_kv + s * KV_WIDTH:c_kv + (s + 1) * KV_WIDTH] for s in range(N_KV_SLOTS)]
        wgate = w[:, c_gate:c_conv].reshape(D_MODEL, 3, N_GROUPS, HEADS_PER_GROUP).transpose(0, 2, 1, 3)
        wksel = jnp.concatenate([
            kv[2].reshape(D_MODEL, N_GROUPS, HEAD_DIM),
            jnp.zeros((D_MODEL, N_GROUPS, GATE_LANE0 - HEAD_DIM), BF16),
            wgate.reshape(D_MODEL, N_GROUPS, 3 * HEADS_PER_GROUP),
            jnp.zeros((D_MODEL, N_GROUPS, SLAB - GATE_LANE0 - 3 * HEADS_PER_GROUP), BF16)],
            axis=2).reshape(D_MODEL, KV_PAD)
        w_b = jnp.concatenate([
            _pad_heads(w[:, :ATTN_WIDTH], N_HEADS), wksel, _pad_heads(kv[4], N_GROUPS), kv[3], kv[5],
            kv[0], kv[1], w[:, c_conv:c_merge]], axis=1)
        q, ksel, kwin, vsel, vwin, kcmp, vcmp, gates, cu, bg = _inproj(
            xs, mix_norm_g[l][None], w_b, _pad_lanes(q_norm_g[l][None]), _pad_lanes(k_norm_g[l][1:3]),
            cos_t, sin_lo, sin_hi, onehot, seq)

        w2k = _pad_lanes(cmp_k_w2[l]).astype(BF16)
        w2v = _pad_lanes(cmp_v_w2[l]).astype(BF16)
        kc, vc = _compress(kcmp, vcmp, _expand_pe(cmp_pe_k[l]), _expand_pe(cmp_pe_v[l]),
                           _expand_cmp_w1(cmp_k_w1[l]), _expand_cmp_w1(cmp_v_w1[l]), w2k, w2v,
                           _pad_lanes(k_norm_g[l][0:1]))

        a = _attention(q, kc, vc, ksel, vsel, kwin, vwin, gates, ovl, batch, seq)

        xs = _merge(xs, a, cu, bg, mix_norm_g[l][None], w[:, c_merge:], w_attn_branch[l].astype(BF16),
                    w_conv_branch[l].astype(BF16), w_out[l].astype(BF16), conv_w[l], seq)

        xs = _ffn(xs, ffn2_norm_g[l][None], ffn2_w_gate[l].astype(BF16), ffn2_w_up[l].astype(BF16),
                  ffn2_w_down[l].astype(BF16))
    return xs.reshape(batch, seq, D_MODEL)
```
